```python
import jax, jax.numpy as jnp
from jax import lax
import numpy as np

D_MODEL = 2048
BATCH = 16
SEQ = 2048
DEPTH = 4
DEC_BATCH = 8
DEC_SEQ = 64
PAST_LEN = 4096

CHUNK = 64
N_META = 16
N_MIXERS = 2
N_RET = (DEPTH + 1) // 2
N_CONV = DEPTH // 2
N_HEADS = 8
HEAD_DK = D_MODEL // N_HEADS
HEAD_DV = 2 * HEAD_DK
CONV_WIDTH = 31
D_FF = 4 * D_MODEL
ROPE_BASE = 10000.0
EPS = 1e-6

kernel_name = "retnet_conformer_stream_step"


def rmsnorm(x, g):
    xf = x.astype(jnp.float32)
    y = xf * lax.rsqrt(jnp.mean(xf * xf, axis=-1, keepdims=True) + EPS)
    return (y * g.astype(jnp.float32)).astype(x.dtype)


def layernorm(x, g, b):
    xf = x.astype(jnp.float32)
    xc = xf - jnp.mean(xf, axis=-1, keepdims=True)
    y = xc * lax.rsqrt(jnp.mean(xc * xc, axis=-1, keepdims=True) + EPS)
    return (y * g.astype(jnp.float32) + b.astype(jnp.float32)).astype(x.dtype)


def log_decay():
    return jnp.log1p(-jnp.exp2(-5.0 - jnp.arange(N_HEADS, dtype=jnp.float32)))


def rope(x, pos):
    half = HEAD_DK // 2
    inv = ROPE_BASE ** (-jnp.arange(half, dtype=jnp.float32) / half)
    ang = pos[:, None] * inv[None, :]
    cos = jnp.cos(ang)[None, :, None, :]
    sin = jnp.sin(ang)[None, :, None, :]
    xf = x.astype(jnp.float32)
    x1, x2 = xf[..., :half], xf[..., half:]
    return jnp.concatenate([x1 * cos - x2 * sin, x2 * cos + x1 * sin], axis=-1).astype(x.dtype)


def retention_chunk(S, q, k, v):
    L = q.shape[1]
    dt = q.dtype
    lg = log_decay()
    idx = jnp.arange(L, dtype=jnp.float32)
    dmat = jnp.exp(lg[:, None, None] * jnp.abs(idx[:, None] - idx[None, :])).astype(dt)
    scores = jnp.einsum('bihd,bjhd->bhij', q, k) * dmat[None]
    inner = jnp.einsum('bhij,bjhe->bihe', scores, v)
    q_dec = jnp.exp(lg[None, :] * (idx[:, None] + 1.0)).astype(dt)
    cross = jnp.einsum('bihd,bhde->bihe', q, S.astype(dt)) * q_dec[None, :, :, None]
    k_dec = jnp.exp(lg[None, :] * (L - 1.0 - idx)[:, None]).astype(dt)
    s_dec = jnp.exp(lg * L).astype(S.dtype)
    S_new = S * s_dec[None, :, None, None] + jnp.einsum(
        'bjhd,bjhe->bhde', k * k_dec[None, :, :, None], v).astype(S.dtype)
    return S_new, inner + cross


def retention_scan(S, q, k, v, L):
    B, T = q.shape[0], q.shape[1]
    nc = T // L

    def blocks(a):
        return jnp.moveaxis(a.reshape(B, nc, L, *a.shape[2:]), 1, 0)

    S, o = lax.scan(lambda s, xs: retention_chunk(s, xs[0], xs[1], xs[2]), S,
                    (blocks(q), blocks(k), blocks(v)))
    o = jnp.moveaxis(o, 0, 1).reshape(B, T, N_HEADS, HEAD_DV)
    return S, o


def retention_mixer(h, S, pos, lead, w_q, w_k, w_v, w_g, w_o, g_ret):
    B, T, _ = h.shape
    q = rope((h @ w_q).reshape(B, T, N_HEADS, HEAD_DK), pos)
    k = rope((h @ w_k).reshape(B, T, N_HEADS, HEAD_DK), pos) * (HEAD_DK ** -0.5)
    v = (h @ w_v).reshape(B, T, N_HEADS, HEAD_DV)
    if lead > 0:
        S, o_lead = retention_scan(S, q[:, :lead], k[:, :lead], v[:, :lead], lead)
        S, o_rest = retention_scan(S, q[:, lead:], k[:, lead:], v[:, lead:], CHUNK)
        o = jnp.concatenate([o_lead, o_rest], axis=1)
    else:
        S, o = retention_scan(S, q, k, v, T)
    o = rmsnorm(o, g_ret.reshape(N_HEADS, HEAD_DV)).reshape(B, T, N_HEADS * HEAD_DV)
    return (jax.nn.silu(h @ w_g) * o) @ w_o, S


def conv_mixer(h, buf, w_pw1, b_pw1, w_dw, b_dw, ln_g, ln_b, w_pw2, b_pw2):
    a = h @ w_pw1 + b_pw1
    u = a[..., :D_MODEL] * jax.nn.sigmoid(a[..., D_MODEL:])
    full = jnp.concatenate([buf.astype(u.dtype), u], axis=1)
    c = lax.conv_general_dilated(full, w_dw[:, None, :].astype(u.dtype), (1,), 'VALID',
                                 dimension_numbers=('NWC', 'WIO', 'NWC'),
                                 feature_group_count=D_MODEL) + b_dw
    c = jax.nn.silu(layernorm(c, ln_g, ln_b))
    return c @ w_pw2 + b_pw2, full[:, -(CONV_WIDTH - 1):]


def sqrelu_mlp(h, w_up, w_down):
    r = jax.nn.relu(h @ w_up)
    return (r * r) @ w_down


def trunk(x, ret_states, conv_bufs, pos, lead, norm_mix, w_q, w_k, w_v, w_g, w_o, g_ret,
          w_pw1, b_pw1, w_dw, b_dw, ln_g, ln_b, w_pw2, b_pw2, norm_mlp, w_up, w_down, norm_final):
    new_ret, new_conv = [], []
    for l in range(DEPTH):
        h = rmsnorm(x, norm_mix[l])
        if l % N_MIXERS == 0:
            r = l // N_MIXERS
            out, s = retention_mixer(h, ret_states[r], pos, lead, w_q[r], w_k[r], w_v[r],
                                     w_g[r], w_o[r], g_ret[r])
            new_ret.append(s)
        else:
            c = l // N_MIXERS
            out, b = conv_mixer(h, conv_bufs[c], w_pw1[c], b_pw1[c], w_dw[c], b_dw[c],
                                ln_g[c], ln_b[c], w_pw2[c], b_pw2[c])
            new_conv.append(b)
        x = x + out
        x = x + sqrelu_mlp(rmsnorm(x, norm_mlp[l]), w_up[l], w_down[l])
    y = rmsnorm(x, norm_final)[:, lead:]
    return y, jnp.stack(new_ret), jnp.stack(new_conv)


def setup_inputs(seed: int = 0) -> dict:
    key = jax.random.key(seed)
    ks = jax.random.split(key, 24)

    def nrm(k, shape, scale):
        return jax.random.normal(k, shape, jnp.float32) * scale

    HV = N_HEADS * HEAD_DV
    return {
        "x_prompt": nrm(ks[0], (BATCH, SEQ, D_MODEL), 1.0),
        "x_sample": nrm(ks[1], (DEC_BATCH, DEC_SEQ, D_MODEL), 1.0),
        "state_ret": nrm(ks[2], (N_RET, DEC_BATCH, N_HEADS, HEAD_DK, HEAD_DV), 0.05),
        "cache_conv": nrm(ks[3], (N_CONV, DEC_BATCH, CONV_WIDTH - 1, D_MODEL), 0.5),
        "meta": nrm(ks[4], (N_META, D_MODEL), 1.0),
        "norm_mix": 1.0 + nrm(ks[5], (DEPTH, D_MODEL), 0.02),
        "w_q": nrm(ks[6], (N_RET, D_MODEL, N_HEADS * HEAD_DK), D_MODEL ** -0.5),
        "w_k": nrm(ks[7], (N_RET, D_MODEL, N_HEADS * HEAD_DK), D_MODEL ** -0.5),
        "w_v": nrm(ks[8], (N_RET, D_MODEL, HV), D_MODEL ** -0.5),
        "w_g": nrm(ks[9], (N_RET, D_MODEL, HV), D_MODEL ** -0.5),
        "w_o": nrm(ks[10], (N_RET, HV, D_MODEL), HV ** -0.5),
        "g_ret": 1.0 + nrm(ks[11], (N_RET, HV), 0.02),
        "w_pw1": nrm(ks[12], (N_CONV, D_MODEL, 2 * D_MODEL), D_MODEL ** -0.5),
        "b_pw1": nrm(ks[13], (N_CONV, 2 * D_MODEL), 0.02),
        "w_dw": nrm(ks[14], (N_CONV, CONV_WIDTH, D_MODEL), CONV_WIDTH ** -0.5),
        "b_dw": nrm(ks[15], (N_CONV, D_MODEL), 0.02),
        "ln_g": 1.0 + nrm(ks[16], (N_CONV, D_MODEL), 0.02),
        "ln_b": nrm(ks[17], (N_CONV, D_MODEL), 0.02),
        "w_pw2": nrm(ks[18], (N_CONV, D_MODEL, D_MODEL), D_MODEL ** -0.5),
        "b_pw2": nrm(ks[19], (N_CONV, D_MODEL), 0.02),
        "norm_mlp": 1.0 + nrm(ks[20], (DEPTH, D_MODEL), 0.02),
        "w_up": nrm(ks[21], (DEPTH, D_MODEL, D_FF), D_MODEL ** -0.5),
        "w_down": nrm(ks[22], (DEPTH, D_FF, D_MODEL), D_FF ** -0.5),
        "norm_final": 1.0 + nrm(ks[23], (D_MODEL,), 0.02),
    }


def reference(x_prompt, x_sample, state_ret, cache_conv, meta, norm_mix, w_q, w_k, w_v, w_g, w_o,
              g_ret, w_pw1, b_pw1, w_dw, b_dw, ln_g, ln_b, w_pw2, b_pw2, norm_mlp, w_up, w_down,
              norm_final):
    weights = (norm_mix, w_q, w_k, w_v, w_g, w_o, g_ret, w_pw1, b_pw1, w_dw, b_dw, ln_g, ln_b,
               w_pw2, b_pw2, norm_mlp, w_up, w_down, norm_final)

    B = x_prompt.shape[0]
    xp = jnp.concatenate(
        [jnp.broadcast_to(meta[None].astype(x_prompt.dtype), (B, N_META, D_MODEL)), x_prompt], axis=1)
    pos_p = jnp.arange(N_META + x_prompt.shape[1], dtype=jnp.float32)
    ret0 = jnp.zeros((N_RET, B, N_HEADS, HEAD_DK, HEAD_DV), x_prompt.dtype)
    conv0 = jnp.zeros((N_CONV, B, CONV_WIDTH - 1, D_MODEL), x_prompt.dtype)
    y_prompt, new_state_ret_p, new_cache_conv_p = trunk(xp, ret0, conv0, pos_p, N_META, *weights)

    pos_s = (PAST_LEN + N_META + jnp.arange(x_sample.shape[1])).astype(jnp.float32)
    y_sample, new_state_ret_s, new_cache_conv_s = trunk(x_sample, state_ret, cache_conv, pos_s, 0,
                                                        *weights)
    return (y_prompt, y_sample, new_state_ret_p, new_cache_conv_p, new_state_ret_s, new_cache_conv_s)
```

```python
import functools

import jax
import jax.numpy as jnp
from jax import lax
from jax.experimental import pallas as pl
from jax.experimental.pallas import tpu as pltpu

D_MODEL = 2048
DEPTH = 4
CHUNK = 64
N_META = 16
N_HEADS = 8
HEAD_DK = D_MODEL // N_HEADS
HEAD_DV = 2 * HEAD_DK
HALF_DK = HEAD_DK // 2
HV = N_HEADS * HEAD_DV
CONV_WIDTH = 31
CONV_TAIL = CONV_WIDTH - 1
D_FF = 4 * D_MODEL
ROPE_BASE = 10000.0
EPS = 1e-6
PAST_LEN = 4096

LANES = 128
WIN_PAD = 32
VMEM_LIMIT = 56 * 1024 * 1024

F32 = jnp.float32
BF16 = jnp.bfloat16


def _params(*sem):
    return pltpu.CompilerParams(dimension_semantics=sem, vmem_limit_bytes=VMEM_LIMIT)


def _rms(x, g):
    return x * lax.rsqrt(jnp.mean(x * x, axis=-1, keepdims=True) + EPS) * g


def _qkvg_kernel(x_ref, g_ref, w_ref, cos_ref, sin_ref, o_ref, h_ref, *, tn):
    j = pl.program_id(1)
    n_q = D_MODEL // tn
    n_qk = 2 * n_q
    n_qkv = n_qk + HV // tn

    @pl.when(j == 0)
    def _():
        h_ref[...] = _rms(x_ref[...], g_ref[...]).astype(BF16)

    acc = jnp.dot(h_ref[...], w_ref[...], preferred_element_type=F32)

    @pl.when(j < n_qk)
    def _():
        scale = jnp.where(j < n_q, 1.0, HEAD_DK ** -0.5).astype(F32)
        cos = cos_ref[...]
        sin = sin_ref[...]
        for hh in range(tn // HEAD_DK):
            a = hh * HEAD_DK
            x1 = acc[:, a:a + HALF_DK]
            x2 = acc[:, a + HALF_DK:a + HEAD_DK]
            o_ref[:, a:a + HALF_DK] = ((x1 * cos - x2 * sin) * scale).astype(BF16)
            o_ref[:, a + HALF_DK:a + HEAD_DK] = ((x2 * cos + x1 * sin) * scale).astype(BF16)

    @pl.when((j >= n_qk) & (j < n_qkv))
    def _():
        o_ref[...] = acc.astype(BF16)

    @pl.when(j >= n_qkv)
    def _():
        o_ref[...] = (acc * jax.nn.sigmoid(acc)).astype(BF16)


def _qkvg(x, gain, w, cos, sin, *, tm, tn):
    T = x.shape[0]
    N = w.shape[1]
    n_pos_blocks = cos.shape[0] // tm
    return pl.pallas_call(
        functools.partial(_qkvg_kernel, tn=tn),
        out_shape=jax.ShapeDtypeStruct((T, N), BF16),
        grid=(T // tm, N // tn),
        in_specs=[
            pl.BlockSpec((tm, D_MODEL), lambda i, j: (i, 0)),
            pl.BlockSpec((1, D_MODEL), lambda i, j: (0, 0)),
            pl.BlockSpec((D_MODEL, tn), lambda i, j: (0, j)),
            pl.BlockSpec((tm, HALF_DK), lambda i, j: (i % n_pos_blocks, 0)),
            pl.BlockSpec((tm, HALF_DK), lambda i, j: (i % n_pos_blocks, 0)),
        ],
        out_specs=pl.BlockSpec((tm, tn), lambda i, j: (i, j)),
        scratch_shapes=[pltpu.VMEM((tm, D_MODEL), BF16)],
        compiler_params=_params("parallel", "arbitrary"),
        name="qkvg_proj",
    )(x, gain, w, cos, sin)


def _retention_kernel(q_ref, k_ref, v_ref, gate_ref, s0_ref, d_ref, qd_ref, kd_ref, sd_ref, gr_ref,
                      z_ref, s_ref, *, nblk, bs):
    s_ref[0, 0] = s0_ref[0, 0]
    dmat = d_ref[0]
    qd = jnp.concatenate([qd_ref[0]] * (HEAD_DV // LANES), axis=-1)
    kd = jnp.concatenate([kd_ref[0]] * (HEAD_DK // LANES), axis=-1)
    sd = sd_ref[0]
    gr = gr_ref[0]

    def block(r0):
        rows = pl.ds(r0, bs)
        q = q_ref[rows, :]
        k = k_ref[rows, :]
        v = v_ref[rows, :]
        S = s_ref[0, 0]
        s = lax.dot_general(q, k, (((1,), (1,)), ((), ())), preferred_element_type=F32)
        p = (s * dmat).astype(BF16)
        o = jnp.dot(p, v, preferred_element_type=F32)
        o = o + jnp.dot(q, S.astype(BF16), preferred_element_type=F32) * qd
        kk = (k.astype(F32) * kd).astype(BF16)
        upd = lax.dot_general(kk, v, (((0,), (0,)), ((), ())), preferred_element_type=F32)
        s_ref[0, 0] = S * sd + upd
        z_ref[rows, :] = (gate_ref[rows, :].astype(F32) * _rms(o, gr)).astype(BF16)

    if nblk == 1:
        block(0)
    else:
        def body(c, carry):
            block(pl.multiple_of(c * bs, bs))
            return carry
        lax.fori_loop(0, nblk, body, 0)


def _retention(qkvg, s0, tabs, g_ret, *, nb, seq, bs, row0):
    dmat, qd, kd, sd = tabs
    rb0 = row0 // seq
    k_col0 = D_MODEL // HEAD_DK
    v_col0 = 2 * D_MODEL // HEAD_DV
    g_col0 = v_col0 + N_HEADS
    s0_map = (lambda b, h: (b, h, 0, 0)) if s0.shape[0] == nb else (lambda b, h: (0, h, 0, 0))
    z, s_new = pl.pallas_call(
        functools.partial(_retention_kernel, nblk=seq // bs, bs=bs),
        out_shape=(jax.ShapeDtypeStruct((nb * seq, HV), BF16),
                   jax.ShapeDtypeStruct((nb, N_HEADS, HEAD_DK, HEAD_DV), F32)),
        grid=(nb, N_HEADS),
        in_specs=[
            pl.BlockSpec((seq, HEAD_DK), lambda b, h: (rb0 + b, h)),
            pl.BlockSpec((seq, HEAD_DK), lambda b, h: (rb0 + b, k_col0 + h)),
            pl.BlockSpec((seq, HEAD_DV), lambda b, h: (rb0 + b, v_col0 + h)),
            pl.BlockSpec((seq, HEAD_DV), lambda b, h: (rb0 + b, g_col0 + h)),
            pl.BlockSpec((1, 1, HEAD_DK, HEAD_DV), s0_map),
            pl.BlockSpec((1, bs, bs), lambda b, h: (h, 0, 0)),
            pl.BlockSpec((1, bs, LANES), lambda b, h: (h, 0, 0)),
            pl.BlockSpec((1, bs, LANES), lambda b, h: (h, 0, 0)),
            pl.BlockSpec((1, 1, HEAD_DV), lambda b, h: (h, 0, 0)),
            pl.BlockSpec((1, 1, HEAD_DV), lambda b, h: (h, 0, 0)),
        ],
        out_specs=(pl.BlockSpec((seq, HEAD_DV), lambda b, h: (b, h)),
                   pl.BlockSpec((1, 1, HEAD_DK, HEAD_DV), lambda b, h: (b, h, 0, 0))),
        compiler_params=_params("parallel", "parallel"),
        name="retention",
    )(qkvg, qkvg, qkvg, qkvg, s0, dmat, qd, kd, sd, g_ret)
    return z, s_new


def _decay_tables(bs, chunk):
    lg = jnp.log1p(-jnp.exp2(-5.0 - jnp.arange(N_HEADS, dtype=F32)))[:, None, None]
    idx = jnp.arange(bs, dtype=F32)
    dist = idx[:, None] - idx[None, :]
    ci = jnp.arange(bs) // chunk
    visible = ci[None, :] <= ci[:, None]
    same = ci[None, :] == ci[:, None]
    dmat = jnp.where(visible[None], jnp.exp(lg * jnp.where(same, jnp.abs(dist), dist)[None]), 0.0)
    qd = jnp.broadcast_to(jnp.exp(lg * (idx[None, :, None] + 1.0)), (N_HEADS, bs, LANES))
    kd = jnp.broadcast_to(jnp.exp(lg * (bs - 1.0 - idx)[None, :, None]), (N_HEADS, bs, LANES))
    sd = jnp.broadcast_to(jnp.exp(lg * bs), (N_HEADS, 1, HEAD_DV))
    return dmat.astype(F32), qd.astype(F32), kd.astype(F32), sd.astype(F32)


def _matmul_residual_kernel(a_ref, w_ref, x_ref, o_ref):
    o_ref[...] = x_ref[...] + jnp.dot(a_ref[...], w_ref[...], preferred_element_type=F32)


def _matmul_residual(a, w, x, *, tm, tn):
    T, K = a.shape
    N = w.shape[1]
    return pl.pallas_call(
        _matmul_residual_kernel,
        out_shape=jax.ShapeDtypeStruct((T, N), F32),
        grid=(T // tm, N // tn),
        in_specs=[
            pl.BlockSpec((tm, K), lambda i, j: (i, 0)),
            pl.BlockSpec((K, tn), lambda i, j: (0, j)),
            pl.BlockSpec((tm, tn), lambda i, j: (i, j)),
        ],
        out_specs=pl.BlockSpec((tm, tn), lambda i, j: (i, j)),
        compiler_params=_params("parallel", "arbitrary"),
        name="out_proj",
    )(a, w, x)


def _mlp_kernel(x_ref, g_ref, wu_ref, wd_ref, gf_ref, o_ref, h_ref, *, final_norm):
    f = pl.program_id(1)

    @pl.when(f == 0)
    def _():
        x = x_ref[...]
        h_ref[...] = _rms(x, g_ref[...]).astype(BF16)
        o_ref[...] = x

    u = jnp.dot(h_ref[...], wu_ref[...], preferred_element_type=F32)
    r = jnp.maximum(u, 0.0)
    o_ref[...] += jnp.dot((r * r).astype(BF16), wd_ref[...], preferred_element_type=F32)

    if final_norm:
        @pl.when(f == pl.num_programs(1) - 1)
        def _():
            o_ref[...] = _rms(o_ref[...], gf_ref[...])


def _mlp(x, gain, w_up, w_down, gain_final, *, tm, tf, final_norm):
    T = x.shape[0]
    return pl.pallas_call(
        functools.partial(_mlp_kernel, final_norm=final_norm),
        out_shape=jax.ShapeDtypeStruct((T, D_MODEL), F32),
        grid=(T // tm, D_FF // tf),
        in_specs=[
            pl.BlockSpec((tm, D_MODEL), lambda i, f: (i, 0)),
            pl.BlockSpec((1, D_MODEL), lambda i, f: (0, 0)),
            pl.BlockSpec((D_MODEL, tf), lambda i, f: (0, f)),
            pl.BlockSpec((tf, D_MODEL), lambda i, f: (f, 0)),
            pl.BlockSpec((1, D_MODEL), lambda i, f: (0, 0)),
        ],
        out_specs=pl.BlockSpec((tm, D_MODEL), lambda i, f: (i, 0)),
        scratch_shapes=[pltpu.VMEM((tm, D_MODEL), BF16)],
        compiler_params=_params("parallel", "arbitrary"),
        name="mlp",
    )(x, gain, w_up, w_down, gain_final)


def _glu_kernel(x_ref, g_ref, wa_ref, wg_ref, ba_ref, bg_ref, o_ref, h_ref):
    @pl.when(pl.program_id(1) == 0)
    def _():
        h_ref[...] = _rms(x_ref[...], g_ref[...]).astype(BF16)

    h = h_ref[...]
    a = jnp.dot(h, wa_ref[...], preferred_element_type=F32) + ba_ref[...]
    g = jnp.dot(h, wg_ref[...], preferred_element_type=F32) + bg_ref[...]
    o_ref[...] = a * jax.nn.sigmoid(g)


def _glu(x, gain, w, b, *, tm, tn):
    T = x.shape[0]
    nj = D_MODEL // tn
    return pl.pallas_call(
        _glu_kernel,
        out_shape=jax.ShapeDtypeStruct((T, D_MODEL), F32),
        grid=(T // tm, nj),
        in_specs=[
            pl.BlockSpec((tm, D_MODEL), lambda i, j: (i, 0)),
            pl.BlockSpec((1, D_MODEL), lambda i, j: (0, 0)),
            pl.BlockSpec((D_MODEL, tn), lambda i, j: (0, j)),
            pl.BlockSpec((D_MODEL, tn), lambda i, j: (0, nj + j)),
            pl.BlockSpec((1, tn), lambda i, j: (0, j)),
            pl.BlockSpec((1, tn), lambda i, j: (0, nj + j)),
        ],
        out_specs=pl.BlockSpec((tm, tn), lambda i, j: (i, j)),
        scratch_shapes=[pltpu.VMEM((tm, D_MODEL), BF16)],
        compiler_params=_params("parallel", "arbitrary"),
        name="glu_proj",
    )(x, gain, w, w, b, b)


def _conv_kernel(u_ref, x_ref, tail_ref, wdw_ref, bdw_ref, lng_ref, lnb_ref, w2_ref, b2_ref,
                 o_ref, tail_out_ref, win_ref, c_ref, *, tm, rows):
    t = pl.program_id(1)

    @pl.when(t == 0)
    def _():
        win_ref[0:WIN_PAD - CONV_TAIL, :] = jnp.zeros((WIN_PAD - CONV_TAIL, D_MODEL), F32)
        win_ref[WIN_PAD - CONV_TAIL:WIN_PAD, :] = tail_ref[0]

    @pl.when(t > 0)
    def _():
        win_ref[0:WIN_PAD, :] = win_ref[tm:tm + WIN_PAD, :]

    win_ref[WIN_PAD:WIN_PAD + tm, :] = u_ref[...]

    def col_block(cb, carry):
        cols = pl.ds(pl.multiple_of(cb * LANES, LANES), LANES)
        for r0 in range(0, tm, rows):
            acc = jnp.broadcast_to(bdw_ref[:, cols], (rows, LANES))
            for j in range(CONV_WIDTH):
                start = WIN_PAD + r0 - j
                acc = acc + wdw_ref[CONV_TAIL - j:CONV_WIDTH - j, cols] * win_ref[start:start + rows, cols]
            c_ref[r0:r0 + rows, cols] = acc
        return carry

    lax.fori_loop(0, D_MODEL // LANES, col_block, 0)

    c = c_ref[...]
    xc = c - jnp.mean(c, axis=-1, keepdims=True)
    y = xc * lax.rsqrt(jnp.mean(xc * xc, axis=-1, keepdims=True) + EPS) * lng_ref[...] + lnb_ref[...]
    a = (y * jax.nn.sigmoid(y)).astype(BF16)
    o_ref[...] = x_ref[...] + jnp.dot(a, w2_ref[...], preferred_element_type=F32) + b2_ref[...]

    @pl.when(t == pl.num_programs(1) - 1)
    def _():
        tail_out_ref[0] = win_ref[WIN_PAD + tm - CONV_TAIL:WIN_PAD + tm, :]


def _conv(u, x, tail, w_dw, b_dw, ln_g, ln_b, w2, b2, *, nb, seq, tm, row0):
    nt = seq // tm
    tb0 = row0 // tm
    tail_map = (lambda b, t: (b, 0, 0)) if tail.shape[0] == nb else (lambda b, t: (0, 0, 0))
    vec = pl.BlockSpec((1, D_MODEL), lambda b, t: (0, 0))
    return pl.pallas_call(
        functools.partial(_conv_kernel, tm=tm, rows=min(tm, 64)),
        out_shape=(jax.ShapeDtypeStruct((nb * seq, D_MODEL), F32),
                   jax.ShapeDtypeStruct((nb, CONV_TAIL, D_MODEL), F32)),
        grid=(nb, nt),
        in_specs=[
            pl.BlockSpec((tm, D_MODEL), lambda b, t: (tb0 + b * nt + t, 0)),
            pl.BlockSpec((tm, D_MODEL), lambda b, t: (tb0 + b * nt + t, 0)),
            pl.BlockSpec((1, CONV_TAIL, D_MODEL), tail_map),
            pl.BlockSpec((CONV_WIDTH, D_MODEL), lambda b, t: (0, 0)),
            vec, vec, vec,
            pl.BlockSpec((D_MODEL, D_MODEL), lambda b, t: (0, 0)),
            vec,
        ],
        out_specs=(pl.BlockSpec((tm, D_MODEL), lambda b, t: (b * nt + t, 0)),
                   pl.BlockSpec((1, CONV_TAIL, D_MODEL), lambda b, t: (b, 0, 0))),
        scratch_shapes=[pltpu.VMEM((WIN_PAD + tm, D_MODEL), F32), pltpu.VMEM((tm, D_MODEL), F32)],
        compiler_params=_params("parallel", "arbitrary"),
        name="conv_module",
    )(u, x, tail, w_dw, b_dw, ln_g, ln_b, w2, b2)


def _rope_tables(pos):
    inv = ROPE_BASE ** (-jnp.arange(HALF_DK, dtype=F32) / HALF_DK)
    ang = pos.astype(F32)[:, None] * inv[None, :]
    return jnp.cos(ang), jnp.sin(ang)


def _row(v):
    return v.reshape(1, -1)


def kernel(x_prompt, x_sample, state_ret, cache_conv, meta, norm_mix, w_q, w_k, w_v, w_g, w_o, g_ret,
           w_pw1, b_pw1, w_dw, b_dw, ln_g, ln_b, w_pw2, b_pw2, norm_mlp, w_up, w_down, norm_final):
    B, SEQ, _ = x_prompt.shape
    DB, DSEQ, _ = x_sample.shape
    n_s = DB * DSEQ

    xs = jnp.concatenate([x_sample.reshape(n_s, D_MODEL), meta.astype(F32)], axis=0)
    xp = x_prompt.reshape(B * SEQ, D_MODEL)

    pos_s = jnp.concatenate([
        jnp.tile(PAST_LEN + N_META + jnp.arange(DSEQ), DB), jnp.arange(N_META)])
    rope_s = _rope_tables(pos_s)
    rope_p = _rope_tables(N_META + jnp.arange(SEQ))

    bs_p = 256
    tabs_p = _decay_tables(bs_p, CHUNK)
    tabs_s = _decay_tables(DSEQ, DSEQ)
    tabs_m = _decay_tables(N_META, N_META)
    zero_state = jnp.zeros((1, N_HEADS, HEAD_DK, HEAD_DV), F32)
    zero_tail = jnp.zeros((1, CONV_TAIL, D_MODEL), F32)

    n_small = n_s + N_META
    ret_p, ret_s, conv_p, conv_s = [], [], [], []
    for l in range(DEPTH):
        i = l // 2
        gmix = _row(norm_mix[l])
        if l % 2 == 0:
            w_in = jnp.concatenate([w_q[i], w_k[i], w_v[i], w_g[i]], axis=1).astype(BF16)
            wo = w_o[i].astype(BF16)
            gr = g_ret[i].reshape(N_HEADS, 1, HEAD_DV)

            qs = _qkvg(xs, gmix, w_in, *rope_s, tm=n_small, tn=512)
            z_s, st_s = _retention(qs, state_ret[i], tabs_s, gr, nb=DB, seq=DSEQ, bs=DSEQ, row0=0)
            z_m, st_m = _retention(qs, zero_state, tabs_m, gr, nb=1, seq=N_META, bs=N_META, row0=n_s)
            xs = _matmul_residual(jnp.concatenate([z_s, z_m], axis=0), wo, xs, tm=n_small, tn=512)
            ret_s.append(st_s)

            qp = _qkvg(xp, gmix, w_in, *rope_p, tm=1024, tn=512)
            z_p, st_p = _retention(qp, st_m, tabs_p, gr, nb=B, seq=SEQ, bs=bs_p, row0=0)
            xp = _matmul_residual(z_p, wo, xp, tm=1024, tn=512)
            ret_p.append(st_p)
        else:
            w1 = w_pw1[i].astype(BF16)
            w2 = w_pw2[i].astype(BF16)
            cargs = (w_dw[i], _row(b_dw[i]), _row(ln_g[i]), _row(ln_b[i]), w2, _row(b_pw2[i]))

            us = _glu(xs, gmix, w1, _row(b_pw1[i]), tm=n_small, tn=512)
            x_s, tail_s = _conv(us, xs, cache_conv[i], *cargs, nb=DB, seq=DSEQ, tm=DSEQ, row0=0)
            x_m, tail_m = _conv(us, xs, zero_tail, *cargs, nb=1, seq=N_META, tm=N_META, row0=n_s)
            xs = jnp.concatenate([x_s, x_m], axis=0)
            conv_s.append(tail_s)

            up = _glu(xp, gmix, w1, _row(b_pw1[i]), tm=1024, tn=512)
            xp, tail_p = _conv(up, xp, tail_m, *cargs, nb=B, seq=SEQ, tm=256, row0=0)
            conv_p.append(tail_p)

        wu = w_up[l].astype(BF16)
        wd = w_down[l].astype(BF16)
        last = l == DEPTH - 1
        margs = (_row(norm_mlp[l]), wu, wd, _row(norm_final))
        xs = _mlp(xs, *margs, tm=n_small, tf=1024, final_norm=last)
        xp = _mlp(xp, *margs, tm=512, tf=1024, final_norm=last)

    return (xp.reshape(B, SEQ, D_MODEL), xs[:n_s].reshape(DB, DSEQ, D_MODEL),
            jnp.stack(ret_p), jnp.stack(conv_p), jnp.stack(ret_s), jnp.stack(conv_s))
```

```python
import functools

import jax
import jax.numpy as jnp
from jax import lax
from jax.experimental import pallas as pl
from jax.experimental.pallas import tpu as pltpu

D_MODEL = 2048
DEPTH = 4
CHUNK = 64
N_META = 16
N_HEADS = 8
HEAD_DK = D_MODEL // N_HEADS
HEAD_DV = 2 * HEAD_DK
HALF_DK = HEAD_DK // 2
HV = N_HEADS * HEAD_DV
CONV_WIDTH = 31
CONV_TAIL = CONV_WIDTH - 1
D_FF = 4 * D_MODEL
ROPE_BASE = 10000.0
EPS = 1e-6
PAST_LEN = 4096

LANES = 128
SUBLANES = 8
WIN_PAD = 32
VMEM_LIMIT = 56 * 1024 * 1024

F32 = jnp.float32
BF16 = jnp.bfloat16


def _params(*sem):
    return pltpu.CompilerParams(dimension_semantics=sem, vmem_limit_bytes=VMEM_LIMIT)


def _rms(x, g):
    return x * lax.rsqrt(jnp.mean(x * x, axis=-1, keepdims=True) + EPS) * g


HEAD_COLS = 2 * HEAD_DK + 2 * HEAD_DV


def _qkvg_kernel(x_ref, g_ref, w_ref, cos_ref, sin_ref, o_ref, h_ref):
    @pl.when(pl.program_id(1) == 0)
    def _():
        h_ref[...] = _rms(x_ref[...], g_ref[...]).astype(BF16)

    h = h_ref[...]
    cos = cos_ref[...]
    sin = sin_ref[...]

    for a, scale in ((0, 1.0), (HEAD_DK, HEAD_DK ** -0.5)):
        acc = jnp.dot(h, w_ref[:, a:a + HEAD_DK], preferred_element_type=F32)
        x1 = acc[:, :HALF_DK]
        x2 = acc[:, HALF_DK:]
        o_ref[:, a:a + HALF_DK] = ((x1 * cos - x2 * sin) * scale).astype(BF16)
        o_ref[:, a + HALF_DK:a + HEAD_DK] = ((x2 * cos + x1 * sin) * scale).astype(BF16)

    a = 2 * HEAD_DK
    o_ref[:, a:a + HEAD_DV] = jnp.dot(h, w_ref[:, a:a + HEAD_DV], preferred_element_type=F32).astype(BF16)
    a += HEAD_DV
    acc = jnp.dot(h, w_ref[:, a:a + HEAD_DV], preferred_element_type=F32)
    o_ref[:, a:a + HEAD_DV] = (acc * jax.nn.sigmoid(acc)).astype(BF16)


def _qkvg(x, gain, w, cos, sin, *, tm):
    T = x.shape[0]
    n_pos_blocks = cos.shape[0] // tm
    return pl.pallas_call(
        _qkvg_kernel,
        out_shape=jax.ShapeDtypeStruct((T, N_HEADS * HEAD_COLS), BF16),
        grid=(T // tm, N_HEADS),
        in_specs=[
            pl.BlockSpec((tm, D_MODEL), lambda i, j: (i, 0)),
            pl.BlockSpec((1, D_MODEL), lambda i, j: (0, 0)),
            pl.BlockSpec((D_MODEL, HEAD_COLS), lambda i, j: (0, j)),
            pl.BlockSpec((tm, HALF_DK), lambda i, j: (i % n_pos_blocks, 0)),
            pl.BlockSpec((tm, HALF_DK), lambda i, j: (i % n_pos_blocks, 0)),
        ],
        out_specs=pl.BlockSpec((tm, HEAD_COLS), lambda i, j: (i, j)),
        scratch_shapes=[pltpu.VMEM((tm, D_MODEL), BF16)],
        compiler_params=_params("parallel", "arbitrary"),
        name="qkvg_proj",
    )(x, gain, w, cos, sin)


def _qkvg_weight(wq, wk, wv, wg):
    parts = [wq.reshape(D_MODEL, N_HEADS, HEAD_DK), wk.reshape(D_MODEL, N_HEADS, HEAD_DK),
             wv.reshape(D_MODEL, N_HEADS, HEAD_DV), wg.reshape(D_MODEL, N_HEADS, HEAD_DV)]
    return jnp.concatenate(parts, axis=2).reshape(D_MODEL, N_HEADS * HEAD_COLS).astype(BF16)


def _retention_kernel(q_ref, k_ref, v_ref, gate_ref, s0_ref, d_ref, qd_ref, kd_ref, sd_ref, gr_ref,
                      z_ref, s_ref, *, nblk, bs):
    s_ref[0, 0] = s0_ref[0, 0]
    dmat = d_ref[0]
    qd = jnp.concatenate([qd_ref[0]] * (HEAD_DV // LANES), axis=-1)
    kd = jnp.concatenate([kd_ref[0]] * (HEAD_DK // LANES), axis=-1)
    sd = sd_ref[0]
    gr = gr_ref[0]

    for c in range(nblk):
        rows = pl.ds(c * bs, bs)
        q = q_ref[rows, :]
        k = k_ref[rows, :]
        v = v_ref[rows, :]
        S = s_ref[0, 0]
        s = lax.dot_general(q, k, (((1,), (1,)), ((), ())), preferred_element_type=F32)
        p = (s * dmat).astype(BF16)
        o = jnp.dot(p, v, preferred_element_type=F32)
        o = o + jnp.dot(q, S.astype(BF16), preferred_element_type=F32) * qd
        kk = (k.astype(F32) * kd).astype(BF16)
        upd = lax.dot_general(kk, v, (((0,), (0,)), ((), ())), preferred_element_type=F32)
        s_ref[0, 0] = S * sd + upd
        z_ref[rows, :] = (gate_ref[rows, :].astype(F32) * _rms(o, gr)).astype(BF16)


def _retention(qkvg, s0, tabs, g_ret, *, nb, seq, bs, row0):
    dmat, qd, kd, sd = tabs
    rb0 = row0 // seq
    qk_blocks = HEAD_COLS // HEAD_DK
    vg_blocks = HEAD_COLS // HEAD_DV
    s0_map = (lambda b, h: (b, h, 0, 0)) if s0.shape[0] == nb else (lambda b, h: (0, h, 0, 0))
    z, s_new = pl.pallas_call(
        functools.partial(_retention_kernel, nblk=seq // bs, bs=bs),
        out_shape=(jax.ShapeDtypeStruct((nb * seq, HV), BF16),
                   jax.ShapeDtypeStruct((nb, N_HEADS, HEAD_DK, HEAD_DV), F32)),
        grid=(nb, N_HEADS),
        in_specs=[
            pl.BlockSpec((seq, HEAD_DK), lambda b, h: (rb0 + b, qk_blocks * h)),
            pl.BlockSpec((seq, HEAD_DK), lambda b, h: (rb0 + b, qk_blocks * h + 1)),
            pl.BlockSpec((seq, HEAD_DV), lambda b, h: (rb0 + b, vg_blocks * h + 1)),
            pl.BlockSpec((seq, HEAD_DV), lambda b, h: (rb0 + b, vg_blocks * h + 2)),
            pl.BlockSpec((1, 1, HEAD_DK, HEAD_DV), s0_map),
            pl.BlockSpec((1, bs, bs), lambda b, h: (h, 0, 0)),
            pl.BlockSpec((1, bs, LANES), lambda b, h: (h, 0, 0)),
            pl.BlockSpec((1, bs, LANES), lambda b, h: (h, 0, 0)),
            pl.BlockSpec((1, 1, HEAD_DV), lambda b, h: (h, 0, 0)),
            pl.BlockSpec((1, 1, HEAD_DV), lambda b, h: (h, 0, 0)),
        ],
        out_specs=(pl.BlockSpec((seq, HEAD_DV), lambda b, h: (b, h)),
                   pl.BlockSpec((1, 1, HEAD_DK, HEAD_DV), lambda b, h: (b, h, 0, 0))),
        compiler_params=_params("parallel", "parallel"),
        name="retention",
    )(qkvg, qkvg, qkvg, qkvg, s0, dmat, qd, kd, sd, g_ret)
    return z, s_new


def _decay_tables(bs, chunk):
    lg = jnp.log1p(-jnp.exp2(-5.0 - jnp.arange(N_HEADS, dtype=F32)))[:, None, None]
    idx = jnp.arange(bs, dtype=F32)
    dist = idx[:, None] - idx[None, :]
    ci = jnp.arange(bs) // chunk
    visible = ci[None, :] <= ci[:, None]
    same = ci[None, :] == ci[:, None]
    dmat = jnp.where(visible[None], jnp.exp(lg * jnp.where(same, jnp.abs(dist), dist)[None]), 0.0)
    qd = jnp.broadcast_to(jnp.exp(lg * (idx[None, :, None] + 1.0)), (N_HEADS, bs, LANES))
    kd = jnp.broadcast_to(jnp.exp(lg * (bs - 1.0 - idx)[None, :, None]), (N_HEADS, bs, LANES))
    sd = jnp.broadcast_to(jnp.exp(lg * bs), (N_HEADS, 1, HEAD_DV))
    return dmat.astype(F32), qd.astype(F32), kd.astype(F32), sd.astype(F32)


def _matmul_residual_kernel(a_ref, w_ref, x_ref, o_ref):
    o_ref[...] = x_ref[...] + jnp.dot(a_ref[...], w_ref[...], preferred_element_type=F32)


def _matmul_residual(a, w, x, *, tm, tn):
    T, K = a.shape
    N = w.shape[1]
    return pl.pallas_call(
        _matmul_residual_kernel,
        out_shape=jax.ShapeDtypeStruct((T, N), F32),
        grid=(T // tm, N // tn),
        in_specs=[
            pl.BlockSpec((tm, K), lambda i, j: (i, 0)),
            pl.BlockSpec((K, tn), lambda i, j: (0, j)),
            pl.BlockSpec((tm, tn), lambda i, j: (i, j)),
        ],
        out_specs=pl.BlockSpec((tm, tn), lambda i, j: (i, j)),
        compiler_params=_params("parallel", "arbitrary"),
        name="out_proj",
    )(a, w, x)


def _mlp_kernel(x_ref, g_ref, wu_ref, wd_ref, gf_ref, o_ref, h_ref, *, final_norm):
    f = pl.program_id(1)

    @pl.when(f == 0)
    def _():
        x = x_ref[...]
        h_ref[...] = _rms(x, g_ref[...]).astype(BF16)
        o_ref[...] = x

    u = jnp.dot(h_ref[...], wu_ref[...], preferred_element_type=F32)
    r = jnp.maximum(u, 0.0)
    o_ref[...] += jnp.dot((r * r).astype(BF16), wd_ref[...], preferred_element_type=F32)

    if final_norm:
        @pl.when(f == pl.num_programs(1) - 1)
        def _():
            o_ref[...] = _rms(o_ref[...], gf_ref[...])


def _mlp(x, gain, w_up, w_down, gain_final, *, tm, tf, final_norm):
    T = x.shape[0]
    return pl.pallas_call(
        functools.partial(_mlp_kernel, final_norm=final_norm),
        out_shape=jax.ShapeDtypeStruct((T, D_MODEL), F32),
        grid=(T // tm, D_FF // tf),
        in_specs=[
            pl.BlockSpec((tm, D_MODEL), lambda i, f: (i, 0)),
            pl.BlockSpec((1, D_MODEL), lambda i, f: (0, 0)),
            pl.BlockSpec((D_MODEL, tf), lambda i, f: (0, f)),
            pl.BlockSpec((tf, D_MODEL), lambda i, f: (f, 0)),
            pl.BlockSpec((1, D_MODEL), lambda i, f: (0, 0)),
        ],
        out_specs=pl.BlockSpec((tm, D_MODEL), lambda i, f: (i, 0)),
        scratch_shapes=[pltpu.VMEM((tm, D_MODEL), BF16)],
        compiler_params=_params("parallel", "arbitrary"),
        name="mlp",
    )(x, gain, w_up, w_down, gain_final)


def _glu_kernel(x_ref, g_ref, wa_ref, wg_ref, ba_ref, bg_ref, o_ref, h_ref):
    @pl.when(pl.program_id(1) == 0)
    def _():
        h_ref[...] = _rms(x_ref[...], g_ref[...]).astype(BF16)

    h = h_ref[...]
    a = jnp.dot(h, wa_ref[...], preferred_element_type=F32) + ba_ref[...]
    g = jnp.dot(h, wg_ref[...], preferred_element_type=F32) + bg_ref[...]
    o_ref[...] = a * jax.nn.sigmoid(g)


def _glu(x, gain, w, b, *, tm, tn):
    T = x.shape[0]
    nj = D_MODEL // tn
    return pl.pallas_call(
        _glu_kernel,
        out_shape=jax.ShapeDtypeStruct((T, D_MODEL), F32),
        grid=(T // tm, nj),
        in_specs=[
            pl.BlockSpec((tm, D_MODEL), lambda i, j: (i, 0)),
            pl.BlockSpec((1, D_MODEL), lambda i, j: (0, 0)),
            pl.BlockSpec((D_MODEL, tn), lambda i, j: (0, j)),
            pl.BlockSpec((D_MODEL, tn), lambda i, j: (0, nj + j)),
            pl.BlockSpec((1, tn), lambda i, j: (0, j)),
            pl.BlockSpec((1, tn), lambda i, j: (0, nj + j)),
        ],
        out_specs=pl.BlockSpec((tm, tn), lambda i, j: (i, j)),
        scratch_shapes=[pltpu.VMEM((tm, D_MODEL), BF16)],
        compiler_params=_params("parallel", "arbitrary"),
        name="glu_proj",
    )(x, gain, w, w, b, b)


def _conv_kernel(u_ref, x_ref, tail_ref, wdw_ref, bdw_ref, lng_ref, lnb_ref, w2_ref, b2_ref,
                 o_ref, tail_out_ref, win_ref, c_ref, *, tm, rows):
    t = pl.program_id(1)

    @pl.when(t == 0)
    def _():
        win_ref[0:WIN_PAD - CONV_TAIL, :] = jnp.zeros((WIN_PAD - CONV_TAIL, D_MODEL), F32)
        win_ref[WIN_PAD - CONV_TAIL:WIN_PAD, :] = tail_ref[0]

    @pl.when(t > 0)
    def _():
        win_ref[0:WIN_PAD, :] = win_ref[tm:tm + WIN_PAD, :]

    win_ref[WIN_PAD:WIN_PAD + tm, :] = u_ref[...]

    def col_block(cb, carry):
        cols = pl.ds(pl.multiple_of(cb * LANES, LANES), LANES)
        for r0 in range(0, tm, rows):
            acc = jnp.broadcast_to(bdw_ref[:, cols], (rows, LANES))
            for s in range(SUBLANES):
                part = None
                for j in range(s, CONV_WIDTH, SUBLANES):
                    start = WIN_PAD + r0 - SUBLANES - (j - s)
                    term = wdw_ref[CONV_TAIL - j:CONV_WIDTH - j, cols] * win_ref[start:start + rows + SUBLANES, cols]
                    part = term if part is None else part + term
                if s:
                    part = pltpu.roll(part, s, axis=0)
                acc = acc + part[SUBLANES:, :]
            c_ref[r0:r0 + rows, cols] = acc
        return carry

    lax.fori_loop(0, D_MODEL // LANES, col_block, 0)

    c = c_ref[...]
    xc = c - jnp.mean(c, axis=-1, keepdims=True)
    y = xc * lax.rsqrt(jnp.mean(xc * xc, axis=-1, keepdims=True) + EPS) * lng_ref[...] + lnb_ref[...]
    a = (y * jax.nn.sigmoid(y)).astype(BF16)
    o_ref[...] = x_ref[...] + jnp.dot(a, w2_ref[...], preferred_element_type=F32) + b2_ref[...]

    @pl.when(t == pl.num_programs(1) - 1)
    def _():
        tail_out_ref[0] = win_ref[WIN_PAD + tm - CONV_TAIL:WIN_PAD + tm, :]


def _conv(u, x, tail, w_dw, b_dw, ln_g, ln_b, w2, b2, *, nb, seq, tm, row0):
    nt = seq // tm
    tb0 = row0 // tm
    tail_map = (lambda b, t: (b, 0, 0)) if tail.shape[0] == nb else (lambda b, t: (0, 0, 0))
    vec = pl.BlockSpec((1, D_MODEL), lambda b, t: (0, 0))
    return pl.pallas_call(
        functools.partial(_conv_kernel, tm=tm, rows=min(tm, 128)),
        out_shape=(jax.ShapeDtypeStruct((nb * seq, D_MODEL), F32),
                   jax.ShapeDtypeStruct((nb, CONV_TAIL, D_MODEL), F32)),
        grid=(nb, nt),
        in_specs=[
            pl.BlockSpec((tm, D_MODEL), lambda b, t: (tb0 + b * nt + t, 0)),
            pl.BlockSpec((tm, D_MODEL), lambda b, t: (tb0 + b * nt + t, 0)),
            pl.BlockSpec((1, CONV_TAIL, D_MODEL), tail_map),
            pl.BlockSpec((CONV_WIDTH, D_MODEL), lambda b, t: (0, 0)),
            vec, vec, vec,
            pl.BlockSpec((D_MODEL, D_MODEL), lambda b, t: (0, 0)),
            vec,
        ],
        out_specs=(pl.BlockSpec((tm, D_MODEL), lambda b, t: (b * nt + t, 0)),
                   pl.BlockSpec((1, CONV_TAIL, D_MODEL), lambda b, t: (b, 0, 0))),
        scratch_shapes=[pltpu.VMEM((WIN_PAD + tm, D_MODEL), F32), pltpu.VMEM((tm, D_MODEL), F32)],
        compiler_params=_params("parallel", "arbitrary"),
        name="conv_module",
    )(u, x, tail, w_dw, b_dw, ln_g, ln_b, w2, b2)


def _rope_tables(pos):
    inv = ROPE_BASE ** (-jnp.arange(HALF_DK, dtype=F32) / HALF_DK)
    ang = pos.astype(F32)[:, None] * inv[None, :]
    return jnp.cos(ang), jnp.sin(ang)


def _row(v):
    return v.reshape(1, -1)


def kernel(x_prompt, x_sample, state_ret, cache_conv, meta, norm_mix, w_q, w_k, w_v, w_g, w_o, g_ret,
           w_pw1, b_pw1, w_dw, b_dw, ln_g, ln_b, w_pw2, b_pw2, norm_mlp, w_up, w_down, norm_final):
    B, SEQ, _ = x_prompt.shape
    DB, DSEQ, _ = x_sample.shape
    n_s = DB * DSEQ

    xs = jnp.concatenate([x_sample.reshape(n_s, D_MODEL), meta.astype(F32)], axis=0)
    xp = x_prompt.reshape(B * SEQ, D_MODEL)

    pos_s = jnp.concatenate([
        jnp.tile(PAST_LEN + N_META + jnp.arange(DSEQ), DB), jnp.arange(N_META)])
    rope_s = _rope_tables(pos_s)
    rope_p = _rope_tables(N_META + jnp.arange(SEQ))

    bs_p = 256
    tabs_p = _decay_tables(bs_p, CHUNK)
    tabs_s = _decay_tables(DSEQ, DSEQ)
    tabs_m = _decay_tables(N_META, N_META)
    zero_state = jnp.zeros((1, N_HEADS, HEAD_DK, HEAD_DV), F32)
    zero_tail = jnp.zeros((1, CONV_TAIL, D_MODEL), F32)

    n_small = n_s + N_META
    ret_p, ret_s, conv_p, conv_s = [], [], [], []
    for l in range(DEPTH):
        i = l // 2
        gmix = _row(norm_mix[l])
        if l % 2 == 0:
            w_in = _qkvg_weight(w_q[i], w_k[i], w_v[i], w_g[i])
            wo = w_o[i].astype(BF16)
            gr = g_ret[i].reshape(N_HEADS, 1, HEAD_DV)

            qs = _qkvg(xs, gmix, w_in, *rope_s, tm=n_small)
            z_s, st_s = _retention(qs, state_ret[i], tabs_s, gr, nb=DB, seq=DSEQ, bs=DSEQ, row0=0)
            z_m, st_m = _retention(qs, zero_state, tabs_m, gr, nb=1, seq=N_META, bs=N_META, row0=n_s)
            xs = _matmul_residual(jnp.concatenate([z_s, z_m], axis=0), wo, xs, tm=n_small, tn=512)
            ret_s.append(st_s)

            qp = _qkvg(xp, gmix, w_in, *rope_p, tm=1024)
            z_p, st_p = _retention(qp, st_m, tabs_p, gr, nb=B, seq=SEQ, bs=bs_p, row0=0)
            xp = _matmul_residual(z_p, wo, xp, tm=1024, tn=512)
            ret_p.append(st_p)
        else:
            w1 = w_pw1[i].astype(BF16)
            w2 = w_pw2[i].astype(BF16)
            cargs = (w_dw[i], _row(b_dw[i]), _row(ln_g[i]), _row(ln_b[i]), w2, _row(b_pw2[i]))

            us = _glu(xs, gmix, w1, _row(b_pw1[i]), tm=n_small, tn=512)
            x_s, tail_s = _conv(us, xs, cache_conv[i], *cargs, nb=DB, seq=DSEQ, tm=DSEQ, row0=0)
            x_m, tail_m = _conv(us, xs, zero_tail, *cargs, nb=1, seq=N_META, tm=N_META, row0=n_s)
            xs = jnp.concatenate([x_s, x_m], axis=0)
            conv_s.append(tail_s)

            up = _glu(xp, gmix, w1, _row(b_pw1[i]), tm=1024, tn=512)
            xp, tail_p = _conv(up, xp, tail_m, *cargs, nb=B, seq=SEQ, tm=256, row0=0)
            conv_p.append(tail_p)

        wu = w_up[l].astype(BF16)
        wd = w_down[l].astype(BF16)
        last = l == DEPTH - 1
        margs = (_row(norm_mlp[l]), wu, wd, _row(norm_final))
        xs = _mlp(xs, *margs, tm=n_small, tf=1024, final_norm=last)
        xp = _mlp(xp, *margs, tm=512, tf=1024, final_norm=last)

    return (xp.reshape(B, SEQ, D_MODEL), xs[:n_s].reshape(DB, DSEQ, D_MODEL),
            jnp.stack(ret_p), jnp.stack(conv_p), jnp.stack(ret_s), jnp.stack(conv_s))
```

```python
import functools

import jax
import jax.numpy as jnp
from jax import lax
from jax.experimental import pallas as pl
from jax.experimental.pallas import tpu as pltpu

D_MODEL = 2048
DEPTH = 4
CHUNK = 64
N_META = 16
N_HEADS = 8
HEAD_DK = D_MODEL // N_HEADS
HEAD_DV = 2 * HEAD_DK
HALF_DK = HEAD_DK // 2
HV = N_HEADS * HEAD_DV
CONV_WIDTH = 31
CONV_TAIL = CONV_WIDTH - 1
D_FF = 4 * D_MODEL
ROPE_BASE = 10000.0
EPS = 1e-6
PAST_LEN = 4096

LANES = 128
SUBLANES = 8
WIN_PAD = 32
VMEM_LIMIT = 56 * 1024 * 1024

F32 = jnp.float32
BF16 = jnp.bfloat16


def _params(*sem):
    return pltpu.CompilerParams(dimension_semantics=sem, vmem_limit_bytes=VMEM_LIMIT)


def _rms(x, g):
    return x * lax.rsqrt(jnp.mean(x * x, axis=-1, keepdims=True) + EPS) * g


HEAD_COLS = 2 * HEAD_DK + 2 * HEAD_DV


def _qkvg_kernel(x_ref, g_ref, wq_ref, wk_ref, wv_ref, wg_ref, cos_ref, sin_ref, o_ref, h_ref):
    @pl.when(pl.program_id(1) == 0)
    def _():
        h_ref[...] = _rms(x_ref[...], g_ref[...]).astype(BF16)

    h = h_ref[...]
    cos = cos_ref[...]
    sin = sin_ref[...]

    for a, w_ref, scale in ((0, wq_ref, 1.0), (HEAD_DK, wk_ref, HEAD_DK ** -0.5)):
        acc = jnp.dot(h, w_ref[...], preferred_element_type=F32)
        x1 = acc[:, :HALF_DK]
        x2 = acc[:, HALF_DK:]
        o_ref[:, a:a + HALF_DK] = ((x1 * cos - x2 * sin) * scale).astype(BF16)
        o_ref[:, a + HALF_DK:a + HEAD_DK] = ((x2 * cos + x1 * sin) * scale).astype(BF16)

    a = 2 * HEAD_DK
    o_ref[:, a:a + HEAD_DV] = jnp.dot(h, wv_ref[...], preferred_element_type=F32).astype(BF16)
    a += HEAD_DV
    acc = jnp.dot(h, wg_ref[...], preferred_element_type=F32)
    o_ref[:, a:a + HEAD_DV] = (acc * jax.nn.sigmoid(acc)).astype(BF16)


def _qkvg(x, gain, wq, wk, wv, wg, cos, sin, *, layer, tm):
    T = x.shape[0]
    n_pos_blocks = cos.shape[0] // tm
    wmap = lambda i, j: (layer, 0, j)
    return pl.pallas_call(
        _qkvg_kernel,
        out_shape=jax.ShapeDtypeStruct((T, N_HEADS * HEAD_COLS), BF16),
        grid=(T // tm, N_HEADS),
        in_specs=[
            pl.BlockSpec((tm, D_MODEL), lambda i, j: (i, 0)),
            pl.BlockSpec((1, D_MODEL), lambda i, j: (0, 0)),
            pl.BlockSpec((None, D_MODEL, HEAD_DK), wmap),
            pl.BlockSpec((None, D_MODEL, HEAD_DK), wmap),
            pl.BlockSpec((None, D_MODEL, HEAD_DV), wmap),
            pl.BlockSpec((None, D_MODEL, HEAD_DV), wmap),
            pl.BlockSpec((tm, HALF_DK), lambda i, j: (i % n_pos_blocks, 0)),
            pl.BlockSpec((tm, HALF_DK), lambda i, j: (i % n_pos_blocks, 0)),
        ],
        out_specs=pl.BlockSpec((tm, HEAD_COLS), lambda i, j: (i, j)),
        scratch_shapes=[pltpu.VMEM((tm, D_MODEL), BF16)],
        compiler_params=_params("parallel", "arbitrary"),
        name="qkvg_proj",
    )(x, gain, wq, wk, wv, wg, cos, sin)


def _retention_kernel(q_ref, k_ref, v_ref, gate_ref, s0_ref, d_ref, qd_ref, kd_ref, sd_ref, gr_ref,
                      z_ref, s_ref, *, nblk, bs):
    s_ref[0, 0] = s0_ref[0, 0]
    dmat = d_ref[0]
    qd = jnp.concatenate([qd_ref[0]] * (HEAD_DV // LANES), axis=-1)
    kd = jnp.concatenate([kd_ref[0]] * (HEAD_DK // LANES), axis=-1)
    sd = sd_ref[0]
    gr = gr_ref[0]

    for c in range(nblk):
        rows = pl.ds(c * bs, bs)
        q = q_ref[rows, :]
        k = k_ref[rows, :]
        v = v_ref[rows, :]
        S = s_ref[0, 0]
        s = lax.dot_general(q, k, (((1,), (1,)), ((), ())), preferred_element_type=F32)
        p = (s * dmat).astype(BF16)
        o = jnp.dot(p, v, preferred_element_type=F32)
        o = o + jnp.dot(q, S.astype(BF16), preferred_element_type=F32) * qd
        kk = (k.astype(F32) * kd).astype(BF16)
        upd = lax.dot_general(kk, v, (((0,), (0,)), ((), ())), preferred_element_type=F32)
        s_ref[0, 0] = S * sd + upd
        z_ref[rows, :] = (gate_ref[rows, :].astype(F32) * _rms(o, gr)).astype(BF16)


def _retention(qkvg, s0, tabs, g_ret, *, layer, s0_layer, nb, seq, bs, row0):
    dmat, qd, kd, sd = tabs
    rb0 = row0 // seq
    qk_blocks = HEAD_COLS // HEAD_DK
    vg_blocks = HEAD_COLS // HEAD_DV
    if s0.shape[1] == nb:
        s0_map = lambda b, h: (s0_layer, b, h, 0, 0)
    else:
        s0_map = lambda b, h: (s0_layer, 0, h, 0, 0)
    z, s_new = pl.pallas_call(
        functools.partial(_retention_kernel, nblk=seq // bs, bs=bs),
        out_shape=(jax.ShapeDtypeStruct((nb * seq, HV), BF16),
                   jax.ShapeDtypeStruct((nb, N_HEADS, HEAD_DK, HEAD_DV), F32)),
        grid=(nb, N_HEADS),
        in_specs=[
            pl.BlockSpec((seq, HEAD_DK), lambda b, h: (rb0 + b, qk_blocks * h)),
            pl.BlockSpec((seq, HEAD_DK), lambda b, h: (rb0 + b, qk_blocks * h + 1)),
            pl.BlockSpec((seq, HEAD_DV), lambda b, h: (rb0 + b, vg_blocks * h + 1)),
            pl.BlockSpec((seq, HEAD_DV), lambda b, h: (rb0 + b, vg_blocks * h + 2)),
            pl.BlockSpec((None, 1, 1, HEAD_DK, HEAD_DV), s0_map),
            pl.BlockSpec((1, bs, bs), lambda b, h: (h, 0, 0)),
            pl.BlockSpec((1, bs, LANES), lambda b, h: (h, 0, 0)),
            pl.BlockSpec((1, bs, LANES), lambda b, h: (h, 0, 0)),
            pl.BlockSpec((1, 1, HEAD_DV), lambda b, h: (h, 0, 0)),
            pl.BlockSpec((None, 1, 1, HEAD_DV), lambda b, h: (layer, h, 0, 0)),
        ],
        out_specs=(pl.BlockSpec((seq, HEAD_DV), lambda b, h: (b, h)),
                   pl.BlockSpec((1, 1, HEAD_DK, HEAD_DV), lambda b, h: (b, h, 0, 0))),
        compiler_params=_params("parallel", "parallel"),
        name="retention",
    )(qkvg, qkvg, qkvg, qkvg, s0, dmat, qd, kd, sd, g_ret)
    return z, s_new


def _decay_tables(bs, chunk):
    lg = jnp.log1p(-jnp.exp2(-5.0 - jnp.arange(N_HEADS, dtype=F32)))[:, None, None]
    idx = jnp.arange(bs, dtype=F32)
    dist = idx[:, None] - idx[None, :]
    ci = jnp.arange(bs) // chunk
    visible = ci[None, :] <= ci[:, None]
    same = ci[None, :] == ci[:, None]
    dmat = jnp.where(visible[None], jnp.exp(lg * jnp.where(same, jnp.abs(dist), dist)[None]), 0.0)
    qd = jnp.broadcast_to(jnp.exp(lg * (idx[None, :, None] + 1.0)), (N_HEADS, bs, LANES))
    kd = jnp.broadcast_to(jnp.exp(lg * (bs - 1.0 - idx)[None, :, None]), (N_HEADS, bs, LANES))
    sd = jnp.broadcast_to(jnp.exp(lg * bs), (N_HEADS, 1, HEAD_DV))
    return dmat.astype(F32), qd.astype(F32), kd.astype(F32), sd.astype(F32)


def _matmul_residual_kernel(a_ref, w_ref, x_ref, o_ref):
    o_ref[...] = x_ref[...] + jnp.dot(a_ref[...], w_ref[...], preferred_element_type=F32)


def _matmul_residual(a, w, x, *, layer, tm, tn):
    T, K = a.shape
    N = w.shape[2]
    return pl.pallas_call(
        _matmul_residual_kernel,
        out_shape=jax.ShapeDtypeStruct((T, N), F32),
        grid=(T // tm, N // tn),
        in_specs=[
            pl.BlockSpec((tm, K), lambda i, j: (i, 0)),
            pl.BlockSpec((None, K, tn), lambda i, j: (layer, 0, j)),
            pl.BlockSpec((tm, tn), lambda i, j: (i, j)),
        ],
        out_specs=pl.BlockSpec((tm, tn), lambda i, j: (i, j)),
        compiler_params=_params("parallel", "arbitrary"),
        name="out_proj",
    )(a, w, x)


def _mlp_kernel(x_ref, g_ref, wu_ref, wd_ref, gf_ref, o_ref, h_ref, *, final_norm):
    f = pl.program_id(1)

    @pl.when(f == 0)
    def _():
        x = x_ref[...]
        h_ref[...] = _rms(x, g_ref[...]).astype(BF16)
        o_ref[...] = x

    u = jnp.dot(h_ref[...], wu_ref[...], preferred_element_type=F32)
    r = jnp.maximum(u, 0.0)
    o_ref[...] += jnp.dot((r * r).astype(BF16), wd_ref[...], preferred_element_type=F32)

    if final_norm:
        @pl.when(f == pl.num_programs(1) - 1)
        def _():
            o_ref[...] = _rms(o_ref[...], gf_ref[...])


def _mlp(x, gain, w_up, w_down, gain_final, *, layer, tm, tf, final_norm):
    T = x.shape[0]
    return pl.pallas_call(
        functools.partial(_mlp_kernel, final_norm=final_norm),
        out_shape=jax.ShapeDtypeStruct((T, D_MODEL), F32),
        grid=(T // tm, D_FF // tf),
        in_specs=[
            pl.BlockSpec((tm, D_MODEL), lambda i, f: (i, 0)),
            pl.BlockSpec((1, D_MODEL), lambda i, f: (0, 0)),
            pl.BlockSpec((None, D_MODEL, tf), lambda i, f: (layer, 0, f)),
            pl.BlockSpec((None, tf, D_MODEL), lambda i, f: (layer, f, 0)),
            pl.BlockSpec((1, D_MODEL), lambda i, f: (0, 0)),
        ],
        out_specs=pl.BlockSpec((tm, D_MODEL), lambda i, f: (i, 0)),
        scratch_shapes=[pltpu.VMEM((tm, D_MODEL), BF16)],
        compiler_params=_params("parallel", "arbitrary"),
        name="mlp",
    )(x, gain, w_up, w_down, gain_final)


def _glu_kernel(x_ref, g_ref, wa_ref, wg_ref, ba_ref, bg_ref, o_ref, h_ref):
    @pl.when(pl.program_id(1) == 0)
    def _():
        h_ref[...] = _rms(x_ref[...], g_ref[...]).astype(BF16)

    h = h_ref[...]
    a = jnp.dot(h, wa_ref[...], preferred_element_type=F32) + ba_ref[...]
    g = jnp.dot(h, wg_ref[...], preferred_element_type=F32) + bg_ref[...]
    o_ref[...] = a * jax.nn.sigmoid(g)


def _glu(x, gain, w, b, *, layer, tm, tn):
    T = x.shape[0]
    nj = D_MODEL // tn
    return pl.pallas_call(
        _glu_kernel,
        out_shape=jax.ShapeDtypeStruct((T, D_MODEL), F32),
        grid=(T // tm, nj),
        in_specs=[
            pl.BlockSpec((tm, D_MODEL), lambda i, j: (i, 0)),
            pl.BlockSpec((1, D_MODEL), lambda i, j: (0, 0)),
            pl.BlockSpec((None, D_MODEL, tn), lambda i, j: (layer, 0, j)),
            pl.BlockSpec((None, D_MODEL, tn), lambda i, j: (layer, 0, nj + j)),
            pl.BlockSpec((1, tn), lambda i, j: (0, j)),
            pl.BlockSpec((1, tn), lambda i, j: (0, nj + j)),
        ],
        out_specs=pl.BlockSpec((tm, tn), lambda i, j: (i, j)),
        scratch_shapes=[pltpu.VMEM((tm, D_MODEL), BF16)],
        compiler_params=_params("parallel", "arbitrary"),
        name="glu_proj",
    )(x, gain, w, w, b, b)


def _conv_kernel(u_ref, x_ref, tail_ref, wdw_ref, bdw_ref, lng_ref, lnb_ref, w2_ref, b2_ref,
                 o_ref, tail_out_ref, win_ref, c_ref, *, tm, rows):
    t = pl.program_id(1)
    n_cb = D_MODEL // LANES

    @pl.when(t == 0)
    def _():
        win_ref[:, 0:WIN_PAD - CONV_TAIL, :] = jnp.zeros((n_cb, WIN_PAD - CONV_TAIL, LANES), F32)
        for cb in range(n_cb):
            win_ref[cb, WIN_PAD - CONV_TAIL:WIN_PAD, :] = tail_ref[0, :, cb * LANES:(cb + 1) * LANES]

    @pl.when(t > 0)
    def _():
        win_ref[:, 0:WIN_PAD, :] = win_ref[:, tm:tm + WIN_PAD, :]

    for cb in range(n_cb):
        win_ref[cb, WIN_PAD:WIN_PAD + tm, :] = u_ref[:, cb * LANES:(cb + 1) * LANES]

    def col_block(cb, carry):
        cols = pl.ds(pl.multiple_of(cb * LANES, LANES), LANES)
        groups = rows // SUBLANES
        bias = jnp.broadcast_to(bdw_ref[:, cols], (groups, LANES))
        w = [jnp.broadcast_to(wdw_ref[CONV_TAIL - j:CONV_WIDTH - j, cols], (groups, LANES))
             for j in range(CONV_WIDTH)]
        for r0 in range(0, tm, rows):
            acc = [bias] * SUBLANES
            for d in range(-CONV_TAIL, SUBLANES):
                x = win_ref[cb, pl.ds(WIN_PAD + r0 + d, groups, stride=SUBLANES), :]
                for m in range(max(0, d), min(SUBLANES - 1, d + CONV_TAIL) + 1):
                    acc[m] = acc[m] + w[m - d] * x
            for m in range(SUBLANES):
                c_ref[cb, pl.ds(r0 + m, groups, stride=SUBLANES), :] = acc[m]
        return carry

    lax.fori_loop(0, n_cb, col_block, 0)

    c = jnp.concatenate([c_ref[cb] for cb in range(n_cb)], axis=-1)
    xc = c - jnp.mean(c, axis=-1, keepdims=True)
    y = xc * lax.rsqrt(jnp.mean(xc * xc, axis=-1, keepdims=True) + EPS) * lng_ref[...] + lnb_ref[...]
    a = (y * jax.nn.sigmoid(y)).astype(BF16)
    o_ref[...] = x_ref[...] + jnp.dot(a, w2_ref[...], preferred_element_type=F32) + b2_ref[...]

    @pl.when(t == pl.num_programs(1) - 1)
    def _():
        for cb in range(n_cb):
            tail_out_ref[0, :, cb * LANES:(cb + 1) * LANES] = win_ref[cb, WIN_PAD + tm - CONV_TAIL:WIN_PAD + tm, :]


def _conv(u, x, tail, w_dw, b_dw, ln_g, ln_b, w2, b2, *, layer, tail_layer, nb, seq, tm, row0):
    nt = seq // tm
    tb0 = row0 // tm
    if tail.shape[1] == nb:
        tail_map = lambda b, t: (tail_layer, b, 0, 0)
    else:
        tail_map = lambda b, t: (tail_layer, 0, 0, 0)
    vec = pl.BlockSpec((1, D_MODEL), lambda b, t: (0, 0))
    return pl.pallas_call(
        functools.partial(_conv_kernel, tm=tm, rows=min(tm, 64)),
        out_shape=(jax.ShapeDtypeStruct((nb * seq, D_MODEL), F32),
                   jax.ShapeDtypeStruct((nb, CONV_TAIL, D_MODEL), F32)),
        grid=(nb, nt),
        in_specs=[
            pl.BlockSpec((tm, D_MODEL), lambda b, t: (tb0 + b * nt + t, 0)),
            pl.BlockSpec((tm, D_MODEL), lambda b, t: (tb0 + b * nt + t, 0)),
            pl.BlockSpec((None, 1, CONV_TAIL, D_MODEL), tail_map),
            pl.BlockSpec((CONV_WIDTH, D_MODEL), lambda b, t: (0, 0)),
            vec, vec, vec,
            pl.BlockSpec((None, D_MODEL, D_MODEL), lambda b, t: (layer, 0, 0)),
            vec,
        ],
        out_specs=(pl.BlockSpec((tm, D_MODEL), lambda b, t: (b * nt + t, 0)),
                   pl.BlockSpec((1, CONV_TAIL, D_MODEL), lambda b, t: (b, 0, 0))),
        scratch_shapes=[pltpu.VMEM((D_MODEL // LANES, WIN_PAD + tm, LANES), F32),
                        pltpu.VMEM((D_MODEL // LANES, tm, LANES), F32)],
        compiler_params=_params("parallel", "arbitrary"),
        name="conv_module",
    )(u, x, tail, w_dw, b_dw, ln_g, ln_b, w2, b2)


def _rope_tables(pos):
    inv = ROPE_BASE ** (-jnp.arange(HALF_DK, dtype=F32) / HALF_DK)
    ang = pos.astype(F32)[:, None] * inv[None, :]
    return jnp.cos(ang), jnp.sin(ang)


def _row(v):
    return v.reshape(1, -1)


def kernel(x_prompt, x_sample, state_ret, cache_conv, meta, norm_mix, w_q, w_k, w_v, w_g, w_o, g_ret,
           w_pw1, b_pw1, w_dw, b_dw, ln_g, ln_b, w_pw2, b_pw2, norm_mlp, w_up, w_down, norm_final):
    B, SEQ, _ = x_prompt.shape
    DB, DSEQ, _ = x_sample.shape
    n_s = DB * DSEQ

    xs = jnp.concatenate([x_sample.reshape(n_s, D_MODEL), meta.astype(F32)], axis=0)
    xp = x_prompt.reshape(B * SEQ, D_MODEL)

    pos_s = jnp.concatenate([
        jnp.tile(PAST_LEN + N_META + jnp.arange(DSEQ), DB), jnp.arange(N_META)])
    rope_s = _rope_tables(pos_s)
    rope_p = _rope_tables(N_META + jnp.arange(SEQ))

    bs_p = 256
    tabs_p = _decay_tables(bs_p, CHUNK)
    tabs_s = _decay_tables(DSEQ, DSEQ)
    tabs_m = _decay_tables(N_META, N_META)
    zero_state = jnp.zeros((1, 1, N_HEADS, HEAD_DK, HEAD_DV), F32)
    zero_tail = jnp.zeros((1, 1, CONV_TAIL, D_MODEL), F32)

    wq, wk, wv, wg, wo = (w.astype(BF16) for w in (w_q, w_k, w_v, w_g, w_o))
    w1, w2, wu, wd = (w.astype(BF16) for w in (w_pw1, w_pw2, w_up, w_down))
    gr = g_ret.reshape(-1, N_HEADS, 1, HEAD_DV)

    n_small = n_s + N_META
    ret_p, ret_s, conv_p, conv_s = [], [], [], []
    for l in range(DEPTH):
        i = l // 2
        gmix = _row(norm_mix[l])
        if l % 2 == 0:
            qs = _qkvg(xs, gmix, wq, wk, wv, wg, *rope_s, layer=i, tm=n_small)
            z_s, st_s = _retention(qs, state_ret, tabs_s, gr, layer=i, s0_layer=i,
                                   nb=DB, seq=DSEQ, bs=DSEQ, row0=0)
            z_m, st_m = _retention(qs, zero_state, tabs_m, gr, layer=i, s0_layer=0,
                                   nb=1, seq=N_META, bs=N_META, row0=n_s)
            xs = _matmul_residual(jnp.concatenate([z_s, z_m], axis=0), wo, xs, layer=i, tm=n_small, tn=512)
            ret_s.append(st_s)

            qp = _qkvg(xp, gmix, wq, wk, wv, wg, *rope_p, layer=i, tm=1024)
            z_p, st_p = _retention(qp, st_m[None], tabs_p, gr, layer=i, s0_layer=0,
                                   nb=B, seq=SEQ, bs=bs_p, row0=0)
            xp = _matmul_residual(z_p, wo, xp, layer=i, tm=1024, tn=512)
            ret_p.append(st_p)
        else:
            cargs = (w_dw[i], _row(b_dw[i]), _row(ln_g[i]), _row(ln_b[i]), w2, _row(b_pw2[i]))

            us = _glu(xs, gmix, w1, _row(b_pw1[i]), layer=i, tm=n_small, tn=512)
            x_s, tail_s = _conv(us, xs, cache_conv, *cargs, layer=i, tail_layer=i,
                                nb=DB, seq=DSEQ, tm=DSEQ, row0=0)
            x_m, tail_m = _conv(us, xs, zero_tail, *cargs, layer=i, tail_layer=0,
                                nb=1, seq=N_META, tm=N_META, row0=n_s)
            xs = jnp.concatenate([x_s, x_m], axis=0)
            conv_s.append(tail_s)

            up = _glu(xp, gmix, w1, _row(b_pw1[i]), layer=i, tm=1024, tn=512)
            xp, tail_p = _conv(up, xp, tail_m[None], *cargs, layer=i, tail_layer=0,
                               nb=B, seq=SEQ, tm=256, row0=0)
            conv_p.append(tail_p)

        last = l == DEPTH - 1
        margs = (_row(norm_mlp[l]), wu, wd, _row(norm_final))
        xs = _mlp(xs, *margs, layer=l, tm=n_small, tf=1024, final_norm=last)
        xp = _mlp(xp, *margs, layer=l, tm=512, tf=1024, final_norm=last)

    return (xp.reshape(B, SEQ, D_MODEL), xs[:n_s].reshape(DB, DSEQ, D_MODEL),
            jnp.stack(ret_p), jnp.stack(conv_p), jnp.stack(ret_s), jnp.stack(conv_s))
```

```python
import functools

import jax
import jax.numpy as jnp
from jax import lax
from jax.experimental import pallas as pl
from jax.experimental.pallas import tpu as pltpu

D_MODEL = 2048
DEPTH = 4
CHUNK = 64
N_META = 16
N_HEADS = 8
HEAD_DK = D_MODEL // N_HEADS
HEAD_DV = 2 * HEAD_DK
HALF_DK = HEAD_DK // 2
HV = N_HEADS * HEAD_DV
CONV_WIDTH = 31
CONV_TAIL = CONV_WIDTH - 1
D_FF = 4 * D_MODEL
ROPE_BASE = 10000.0
EPS = 1e-6
PAST_LEN = 4096

LANES = 128
SUBLANES = 8
WIN_PAD = 32
VMEM_LIMIT = 56 * 1024 * 1024

F32 = jnp.float32
BF16 = jnp.bfloat16


def _params(*sem):
    return pltpu.CompilerParams(dimension_semantics=sem, vmem_limit_bytes=VMEM_LIMIT)


def _rms(x, g):
    return x * lax.rsqrt(jnp.mean(x * x, axis=-1, keepdims=True) + EPS) * g


HEAD_COLS = 2 * HEAD_DK + 2 * HEAD_DV


def _weight_spec(w, layer, block, index_map):
    if w.ndim == 3:
        return pl.BlockSpec((None,) + block, lambda *g: (layer,) + index_map(*g))
    return pl.BlockSpec(block, index_map)


def _bf16_copy(w, block, index_map):
    return jax.ShapeDtypeStruct(w.shape[1:], BF16), pl.BlockSpec(block, index_map)


def _load_weights(w_refs, copy_refs):
    ws = [w_ref[...].astype(BF16) for w_ref in w_refs]
    for copy_ref, w in zip(copy_refs, ws):
        copy_ref[...] = w
    return ws


def _qkvg_kernel(x_ref, g_ref, wq_ref, wk_ref, wv_ref, wg_ref, cos_ref, sin_ref, o_ref, *rest):
    h_ref = rest[-1]

    @pl.when(pl.program_id(1) == 0)
    def _():
        h_ref[...] = _rms(x_ref[...], g_ref[...]).astype(BF16)

    wq, wk, wv, wg = _load_weights((wq_ref, wk_ref, wv_ref, wg_ref), rest[:-1])
    h = h_ref[...]
    cos = cos_ref[...]
    sin = sin_ref[...]

    for a, w, scale in ((0, wq, 1.0), (HEAD_DK, wk, HEAD_DK ** -0.5)):
        acc = jnp.dot(h, w, preferred_element_type=F32)
        x1 = acc[:, :HALF_DK]
        x2 = acc[:, HALF_DK:]
        o_ref[:, a:a + HALF_DK] = ((x1 * cos - x2 * sin) * scale).astype(BF16)
        o_ref[:, a + HALF_DK:a + HEAD_DK] = ((x2 * cos + x1 * sin) * scale).astype(BF16)

    a = 2 * HEAD_DK
    o_ref[:, a:a + HEAD_DV] = jnp.dot(h, wv, preferred_element_type=F32).astype(BF16)
    a += HEAD_DV
    acc = jnp.dot(h, wg, preferred_element_type=F32)
    o_ref[:, a:a + HEAD_DV] = (acc * jax.nn.sigmoid(acc)).astype(BF16)


def _qkvg(x, gain, wq, wk, wv, wg, cos, sin, *, layer, tm):
    T = x.shape[0]
    n_pos_blocks = cos.shape[0] // tm
    emit = wq.ndim == 3
    assert not emit or T == tm
    wmap = lambda i, j: (0, j)
    blocks = ((D_MODEL, HEAD_DK), (D_MODEL, HEAD_DK), (D_MODEL, HEAD_DV), (D_MODEL, HEAD_DV))
    ws = (wq, wk, wv, wg)
    copies = [_bf16_copy(w, blk, wmap) for w, blk in zip(ws, blocks)] if emit else []
    res = pl.pallas_call(
        _qkvg_kernel,
        out_shape=[jax.ShapeDtypeStruct((T, N_HEADS * HEAD_COLS), BF16)] + [c[0] for c in copies],
        grid=(T // tm, N_HEADS),
        in_specs=[
            pl.BlockSpec((tm, D_MODEL), lambda i, j: (i, 0)),
            pl.BlockSpec((1, D_MODEL), lambda i, j: (0, 0)),
            *[_weight_spec(w, layer, blk, wmap) for w, blk in zip(ws, blocks)],
            pl.BlockSpec((tm, HALF_DK), lambda i, j: (i % n_pos_blocks, 0)),
            pl.BlockSpec((tm, HALF_DK), lambda i, j: (i % n_pos_blocks, 0)),
        ],
        out_specs=[pl.BlockSpec((tm, HEAD_COLS), lambda i, j: (i, j))] + [c[1] for c in copies],
        scratch_shapes=[pltpu.VMEM((tm, D_MODEL), BF16)],
        compiler_params=_params("parallel", "arbitrary"),
        name="qkvg_proj",
    )(x, gain, wq, wk, wv, wg, cos, sin)
    return res if emit else res[0]


def _retention_kernel(qkvg_ref, s0_ref, d_ref, qd_ref, kd_ref, sd_ref, gr_ref, z_ref, s_ref, *, nblk, bs, heads):
    for hh in range(heads):
        col = hh * HEAD_COLS
        s_ref[0, hh] = s0_ref[0, hh]
        dmat = d_ref[hh]
        qd = jnp.concatenate([qd_ref[hh]] * (HEAD_DV // LANES), axis=-1)
        kd = jnp.concatenate([kd_ref[hh]] * (HEAD_DK // LANES), axis=-1)
        sd = sd_ref[hh]
        gr = gr_ref[hh]

        for c in range(nblk):
            rows = pl.ds(c * bs, bs)
            q = qkvg_ref[rows, col:col + HEAD_DK]
            k = qkvg_ref[rows, col + HEAD_DK:col + 2 * HEAD_DK]
            v = qkvg_ref[rows, col + 2 * HEAD_DK:col + 2 * HEAD_DK + HEAD_DV]
            gate = qkvg_ref[rows, col + 2 * HEAD_DK + HEAD_DV:col + HEAD_COLS]
            S = s_ref[0, hh]
            s = lax.dot_general(q, k, (((1,), (1,)), ((), ())), preferred_element_type=F32)
            p = (s * dmat).astype(BF16)
            o = jnp.dot(p, v, preferred_element_type=F32)
            o = o + jnp.dot(q, S.astype(BF16), preferred_element_type=F32) * qd
            kk = (k.astype(F32) * kd).astype(BF16)
            upd = lax.dot_general(kk, v, (((0,), (0,)), ((), ())), preferred_element_type=F32)
            s_ref[0, hh] = S * sd + upd
            z_ref[rows, hh * HEAD_DV:(hh + 1) * HEAD_DV] = (gate.astype(F32) * _rms(o, gr)).astype(BF16)


def _retention(qkvg, s0, tabs, g_ret, *, layer, s0_layer, nb, seq, bs, row0, heads):
    dmat, qd, kd, sd = tabs
    rb0 = row0 // seq
    if s0.shape[1] == nb:
        s0_map = lambda b, h: (s0_layer, b, h, 0, 0)
    else:
        s0_map = lambda b, h: (s0_layer, 0, h, 0, 0)
    z, s_new = pl.pallas_call(
        functools.partial(_retention_kernel, nblk=seq // bs, bs=bs, heads=heads),
        out_shape=(jax.ShapeDtypeStruct((nb * seq, HV), BF16),
                   jax.ShapeDtypeStruct((nb, N_HEADS, HEAD_DK, HEAD_DV), F32)),
        grid=(nb, N_HEADS // heads),
        in_specs=[
            pl.BlockSpec((seq, heads * HEAD_COLS), lambda b, h: (rb0 + b, h)),
            pl.BlockSpec((None, 1, heads, HEAD_DK, HEAD_DV), s0_map),
            pl.BlockSpec((heads, bs, bs), lambda b, h: (h, 0, 0)),
            pl.BlockSpec((heads, bs, LANES), lambda b, h: (h, 0, 0)),
            pl.BlockSpec((heads, bs, LANES), lambda b, h: (h, 0, 0)),
            pl.BlockSpec((heads, 1, HEAD_DV), lambda b, h: (h, 0, 0)),
            pl.BlockSpec((None, heads, 1, HEAD_DV), lambda b, h: (layer, h, 0, 0)),
        ],
        out_specs=(pl.BlockSpec((seq, heads * HEAD_DV), lambda b, h: (b, h)),
                   pl.BlockSpec((1, heads, HEAD_DK, HEAD_DV), lambda b, h: (b, h, 0, 0))),
        compiler_params=_params("parallel", "parallel"),
        name="retention",
    )(qkvg, s0, dmat, qd, kd, sd, g_ret)
    return z, s_new


def _decay_tables(bs, chunk):
    lg = jnp.log1p(-jnp.exp2(-5.0 - jnp.arange(N_HEADS, dtype=F32)))[:, None, None]
    idx = jnp.arange(bs, dtype=F32)
    dist = idx[:, None] - idx[None, :]
    ci = jnp.arange(bs) // chunk
    visible = ci[None, :] <= ci[:, None]
    same = ci[None, :] == ci[:, None]
    dmat = jnp.where(visible[None], jnp.exp(lg * jnp.where(same, jnp.abs(dist), dist)[None]), 0.0)
    qd = jnp.broadcast_to(jnp.exp(lg * (idx[None, :, None] + 1.0)), (N_HEADS, bs, LANES))
    kd = jnp.broadcast_to(jnp.exp(lg * (bs - 1.0 - idx)[None, :, None]), (N_HEADS, bs, LANES))
    sd = jnp.broadcast_to(jnp.exp(lg * bs), (N_HEADS, 1, HEAD_DV))
    return dmat.astype(F32), qd.astype(F32), kd.astype(F32), sd.astype(F32)


def _matmul_residual_kernel(a_ref, w_ref, x_ref, o_ref, *copy_refs):
    (w,) = _load_weights((w_ref,), copy_refs)
    o_ref[...] = x_ref[...] + jnp.dot(a_ref[...], w, preferred_element_type=F32)


def _matmul_residual(a, w, x, *, layer, tm, tn):
    T, K = a.shape
    N = w.shape[-1]
    emit = w.ndim == 3
    assert not emit or T == tm
    wmap = lambda i, j: (0, j)
    copies = [_bf16_copy(w, (K, tn), wmap)] if emit else []
    res = pl.pallas_call(
        _matmul_residual_kernel,
        out_shape=[jax.ShapeDtypeStruct((T, N), F32)] + [c[0] for c in copies],
        grid=(T // tm, N // tn),
        in_specs=[
            pl.BlockSpec((tm, K), lambda i, j: (i, 0)),
            _weight_spec(w, layer, (K, tn), wmap),
            pl.BlockSpec((tm, tn), lambda i, j: (i, j)),
        ],
        out_specs=[pl.BlockSpec((tm, tn), lambda i, j: (i, j))] + [c[1] for c in copies],
        compiler_params=_params("parallel", "arbitrary"),
        name="out_proj",
    )(a, w, x)
    return res if emit else res[0]


def _mlp_kernel(x_ref, g_ref, wu_ref, wd_ref, gf_ref, o_ref, *rest, final_norm):
    f = pl.program_id(1)
    h_ref = rest[-1]

    @pl.when(f == 0)
    def _():
        x = x_ref[...]
        h_ref[...] = _rms(x, g_ref[...]).astype(BF16)
        o_ref[...] = x

    wu, wd = _load_weights((wu_ref, wd_ref), rest[:-1])
    u = jnp.dot(h_ref[...], wu, preferred_element_type=F32)
    r = jnp.maximum(u, 0.0)
    o_ref[...] += jnp.dot((r * r).astype(BF16), wd, preferred_element_type=F32)

    if final_norm:
        @pl.when(f == pl.num_programs(1) - 1)
        def _():
            o_ref[...] = _rms(o_ref[...], gf_ref[...])


def _mlp(x, gain, w_up, w_down, gain_final, *, layer, tm, tf, final_norm):
    T = x.shape[0]
    emit = w_up.ndim == 3
    assert not emit or T == tm
    up_args = ((D_MODEL, tf), lambda i, f: (0, f))
    down_args = ((tf, D_MODEL), lambda i, f: (f, 0))
    copies = [_bf16_copy(w_up, *up_args), _bf16_copy(w_down, *down_args)] if emit else []
    res = pl.pallas_call(
        functools.partial(_mlp_kernel, final_norm=final_norm),
        out_shape=[jax.ShapeDtypeStruct((T, D_MODEL), F32)] + [c[0] for c in copies],
        grid=(T // tm, D_FF // tf),
        in_specs=[
            pl.BlockSpec((tm, D_MODEL), lambda i, f: (i, 0)),
            pl.BlockSpec((1, D_MODEL), lambda i, f: (0, 0)),
            _weight_spec(w_up, layer, *up_args),
            _weight_spec(w_down, layer, *down_args),
            pl.BlockSpec((1, D_MODEL), lambda i, f: (0, 0)),
        ],
        out_specs=[pl.BlockSpec((tm, D_MODEL), lambda i, f: (i, 0))] + [c[1] for c in copies],
        scratch_shapes=[pltpu.VMEM((tm, D_MODEL), BF16)],
        compiler_params=_params("parallel", "arbitrary"),
        name="mlp",
    )(x, gain, w_up, w_down, gain_final)
    return res if emit else res[0]


def _glu_kernel(x_ref, g_ref, wa_ref, wg_ref, ba_ref, bg_ref, o_ref, *rest):
    h_ref = rest[-1]

    @pl.when(pl.program_id(1) == 0)
    def _():
        h_ref[...] = _rms(x_ref[...], g_ref[...]).astype(BF16)

    wa, wg = _load_weights((wa_ref, wg_ref), rest[:-1])
    h = h_ref[...]
    a = jnp.dot(h, wa, preferred_element_type=F32) + ba_ref[...]
    g = jnp.dot(h, wg, preferred_element_type=F32) + bg_ref[...]
    o_ref[...] = a * jax.nn.sigmoid(g)


def _glu(x, gain, w, b, *, layer, tm, tn):
    T = x.shape[0]
    nj = D_MODEL // tn
    emit = not isinstance(w, tuple)
    assert not emit or T == tm
    half = pl.BlockSpec((D_MODEL, tn), lambda i, j: (0, j))
    if emit:
        w_args = (w, w)
        w_specs = [pl.BlockSpec((None, D_MODEL, tn), lambda i, j: (layer, 0, j)),
                   pl.BlockSpec((None, D_MODEL, tn), lambda i, j: (layer, 0, nj + j))]
        copies = [(jax.ShapeDtypeStruct((D_MODEL, D_MODEL), BF16), half)] * 2
    else:
        w_args = w
        w_specs = [half, half]
        copies = []
    res = pl.pallas_call(
        _glu_kernel,
        out_shape=[jax.ShapeDtypeStruct((T, D_MODEL), F32)] + [c[0] for c in copies],
        grid=(T // tm, nj),
        in_specs=[
            pl.BlockSpec((tm, D_MODEL), lambda i, j: (i, 0)),
            pl.BlockSpec((1, D_MODEL), lambda i, j: (0, 0)),
            *w_specs,
            pl.BlockSpec((1, tn), lambda i, j: (0, j)),
            pl.BlockSpec((1, tn), lambda i, j: (0, nj + j)),
        ],
        out_specs=[pl.BlockSpec((tm, tn), lambda i, j: (i, j))] + [c[1] for c in copies],
        scratch_shapes=[pltpu.VMEM((tm, D_MODEL), BF16)],
        compiler_params=_params("parallel", "arbitrary"),
        name="glu_proj",
    )(x, gain, *w_args, b, b)
    return (res[0], (res[1], res[2])) if emit else res[0]


def _conv_kernel(u_ref, x_ref, tail_ref, wdw_ref, bdw_ref, lng_ref, lnb_ref, w2_ref, b2_ref,
                 o_ref, tail_out_ref, win_ref, c_ref, *, tm, rows):
    t = pl.program_id(1)
    n_cb = D_MODEL // LANES

    @pl.when(t == 0)
    def _():
        win_ref[:, 0:WIN_PAD - CONV_TAIL, :] = jnp.zeros((n_cb, WIN_PAD - CONV_TAIL, LANES), F32)
        for cb in range(n_cb):
            win_ref[cb, WIN_PAD - CONV_TAIL:WIN_PAD, :] = tail_ref[0, :, cb * LANES:(cb + 1) * LANES]

    @pl.when(t > 0)
    def _():
        win_ref[:, 0:WIN_PAD, :] = win_ref[:, tm:tm + WIN_PAD, :]

    for cb in range(n_cb):
        win_ref[cb, WIN_PAD:WIN_PAD + tm, :] = u_ref[:, cb * LANES:(cb + 1) * LANES]

    def col_block(cb, carry):
        cols = pl.ds(pl.multiple_of(cb * LANES, LANES), LANES)
        groups = rows // SUBLANES
        bias = jnp.broadcast_to(bdw_ref[:, cols], (groups, LANES))
        w = [jnp.broadcast_to(wdw_ref[CONV_TAIL - j:CONV_WIDTH - j, cols], (groups, LANES))
             for j in range(CONV_WIDTH)]
        for r0 in range(0, tm, rows):
            acc = [bias] * SUBLANES
            for d in range(-CONV_TAIL, SUBLANES):
                x = win_ref[cb, pl.ds(WIN_PAD + r0 + d, groups, stride=SUBLANES), :]
                for m in range(max(0, d), min(SUBLANES - 1, d + CONV_TAIL) + 1):
                    acc[m] = acc[m] + w[m - d] * x
            for m in range(SUBLANES):
                c_ref[cb, pl.ds(r0 + m, groups, stride=SUBLANES), :] = acc[m]
        return carry

    lax.fori_loop(0, n_cb, col_block, 0)

    c = jnp.concatenate([c_ref[cb] for cb in range(n_cb)], axis=-1)
    xc = c - jnp.mean(c, axis=-1, keepdims=True)
    y = xc * lax.rsqrt(jnp.mean(xc * xc, axis=-1, keepdims=True) + EPS) * lng_ref[...] + lnb_ref[...]
    a = (y * jax.nn.sigmoid(y)).astype(BF16)
    o_ref[...] = x_ref[...] + jnp.dot(a, w2_ref[...], preferred_element_type=F32) + b2_ref[...]

    @pl.when(t == pl.num_programs(1) - 1)
    def _():
        for cb in range(n_cb):
            tail_out_ref[0, :, cb * LANES:(cb + 1) * LANES] = win_ref[cb, WIN_PAD + tm - CONV_TAIL:WIN_PAD + tm, :]


def _conv(u, x, tail, w_dw, b_dw, ln_g, ln_b, w2, b2, *, layer, tail_layer, nb, seq, tm, row0):
    nt = seq // tm
    tb0 = row0 // tm
    if tail.shape[1] == nb:
        tail_map = lambda b, t: (tail_layer, b, 0, 0)
    else:
        tail_map = lambda b, t: (tail_layer, 0, 0, 0)
    vec = pl.BlockSpec((1, D_MODEL), lambda b, t: (0, 0))
    return pl.pallas_call(
        functools.partial(_conv_kernel, tm=tm, rows=min(tm, 64)),
        out_shape=(jax.ShapeDtypeStruct((nb * seq, D_MODEL), F32),
                   jax.ShapeDtypeStruct((nb, CONV_TAIL, D_MODEL), F32)),
        grid=(nb, nt),
        in_specs=[
            pl.BlockSpec((tm, D_MODEL), lambda b, t: (tb0 + b * nt + t, 0)),
            pl.BlockSpec((tm, D_MODEL), lambda b, t: (tb0 + b * nt + t, 0)),
            pl.BlockSpec((None, 1, CONV_TAIL, D_MODEL), tail_map),
            pl.BlockSpec((CONV_WIDTH, D_MODEL), lambda b, t: (0, 0)),
            vec, vec, vec,
            pl.BlockSpec((None, D_MODEL, D_MODEL), lambda b, t: (layer, 0, 0), pipeline_mode=pl.Buffered(1)),
            vec,
        ],
        out_specs=(pl.BlockSpec((tm, D_MODEL), lambda b, t: (b * nt + t, 0)),
                   pl.BlockSpec((1, CONV_TAIL, D_MODEL), lambda b, t: (b, 0, 0))),
        scratch_shapes=[pltpu.VMEM((D_MODEL // LANES, WIN_PAD + tm, LANES), F32),
                        pltpu.VMEM((D_MODEL // LANES, tm, LANES), F32)],
        compiler_params=_params("parallel", "arbitrary"),
        name="conv_module",
    )(u, x, tail, w_dw, b_dw, ln_g, ln_b, w2, b2)


def _rope_tables(pos):
    inv = ROPE_BASE ** (-jnp.arange(HALF_DK, dtype=F32) / HALF_DK)
    ang = pos.astype(F32)[:, None] * inv[None, :]
    return jnp.cos(ang), jnp.sin(ang)


def _row(v):
    return v.reshape(1, -1)


def kernel(x_prompt, x_sample, state_ret, cache_conv, meta, norm_mix, w_q, w_k, w_v, w_g, w_o, g_ret,
           w_pw1, b_pw1, w_dw, b_dw, ln_g, ln_b, w_pw2, b_pw2, norm_mlp, w_up, w_down, norm_final):
    B, SEQ, _ = x_prompt.shape
    DB, DSEQ, _ = x_sample.shape
    n_s = DB * DSEQ

    xs = jnp.concatenate([x_sample.reshape(n_s, D_MODEL), meta.astype(F32)], axis=0)
    xp = x_prompt.reshape(B * SEQ, D_MODEL)

    pos_s = jnp.concatenate([
        jnp.tile(PAST_LEN + N_META + jnp.arange(DSEQ), DB), jnp.arange(N_META)])
    rope_s = _rope_tables(pos_s)
    rope_p = _rope_tables(N_META + jnp.arange(SEQ))

    bs_p = 256
    tabs_p = _decay_tables(bs_p, CHUNK)
    tabs_s = _decay_tables(DSEQ, DSEQ)
    tabs_m = _decay_tables(N_META, N_META)
    zero_state = jnp.zeros((1, 1, N_HEADS, HEAD_DK, HEAD_DV), F32)
    zero_tail = jnp.zeros((1, 1, CONV_TAIL, D_MODEL), F32)

    w2 = w_pw2.astype(BF16)
    gr = g_ret.reshape(-1, N_HEADS, 1, HEAD_DV)

    n_small = n_s + N_META
    ret_p, ret_s, conv_p, conv_s = [], [], [], []
    for l in range(DEPTH):
        i = l // 2
        gmix = _row(norm_mix[l])
        if l % 2 == 0:
            qs, wq, wk, wv, wg = _qkvg(xs, gmix, w_q, w_k, w_v, w_g, *rope_s, layer=i, tm=n_small)
            z_s, st_s = _retention(qs, state_ret, tabs_s, gr, layer=i, s0_layer=i,
                                   nb=DB, seq=DSEQ, bs=DSEQ, row0=0, heads=N_HEADS)
            z_m, st_m = _retention(qs, zero_state, tabs_m, gr, layer=i, s0_layer=0,
                                   nb=1, seq=N_META, bs=N_META, row0=n_s, heads=N_HEADS)
            xs, wo = _matmul_residual(jnp.concatenate([z_s, z_m], axis=0), w_o, xs, layer=i, tm=n_small, tn=512)
            ret_s.append(st_s)

            qp = _qkvg(xp, gmix, wq, wk, wv, wg, *rope_p, layer=i, tm=1024)
            z_p, st_p = _retention(qp, st_m[None], tabs_p, gr, layer=i, s0_layer=0,
                                   nb=B, seq=SEQ, bs=bs_p, row0=0, heads=1)
            xp = _matmul_residual(z_p, wo, xp, layer=i, tm=1024, tn=512)
            ret_p.append(st_p)
        else:
            cargs = (w_dw[i], _row(b_dw[i]), _row(ln_g[i]), _row(ln_b[i]), w2, _row(b_pw2[i]))

            us, w1 = _glu(xs, gmix, w_pw1, _row(b_pw1[i]), layer=i, tm=n_small, tn=512)
            x_s, tail_s = _conv(us, xs, cache_conv, *cargs, layer=i, tail_layer=i,
                                nb=DB, seq=DSEQ, tm=DSEQ, row0=0)
            x_m, tail_m = _conv(us, xs, zero_tail, *cargs, layer=i, tail_layer=0,
                                nb=1, seq=N_META, tm=N_META, row0=n_s)
            xs = jnp.concatenate([x_s, x_m], axis=0)
            conv_s.append(tail_s)

            up = _glu(xp, gmix, w1, _row(b_pw1[i]), layer=i, tm=1024, tn=512)
            xp, tail_p = _conv(up, xp, tail_m[None], *cargs, layer=i, tail_layer=0,
                               nb=B, seq=SEQ, tm=512, row0=0)
            conv_p.append(tail_p)

        last = l == DEPTH - 1
        xs, wu, wd = _mlp(xs, _row(norm_mlp[l]), w_up, w_down, _row(norm_final),
                          layer=l, tm=n_small, tf=512, final_norm=last)
        xp = _mlp(xp, _row(norm_mlp[l]), wu, wd, _row(norm_final), layer=l, tm=512, tf=1024, final_norm=last)

    return (xp.reshape(B, SEQ, D_MODEL), xs[:n_s].reshape(DB, DSEQ, D_MODEL),
            jnp.stack(ret_p), jnp.stack(conv_p), jnp.stack(ret_s), jnp.stack(conv_s))
```

```python
import functools

import jax
import jax.numpy as jnp
from jax import lax
from jax.experimental import pallas as pl
from jax.experimental.pallas import tpu as pltpu

D_MODEL = 2048
DEPTH = 4
CHUNK = 64
N_META = 16
N_HEADS = 8
HEAD_DK = D_MODEL // N_HEADS
HEAD_DV = 2 * HEAD_DK
HALF_DK = HEAD_DK // 2
HV = N_HEADS * HEAD_DV
CONV_WIDTH = 31
CONV_TAIL = CONV_WIDTH - 1
D_FF = 4 * D_MODEL
ROPE_BASE = 10000.0
EPS = 1e-6
PAST_LEN = 4096

LANES = 128
SUBLANES = 8
WIN_PAD = 32
VMEM_LIMIT = 56 * 1024 * 1024

F32 = jnp.float32
BF16 = jnp.bfloat16


def _params(*sem):
    return pltpu.CompilerParams(dimension_semantics=sem, vmem_limit_bytes=VMEM_LIMIT)


def _rms(x, g):
    return x * lax.rsqrt(jnp.mean(x * x, axis=-1, keepdims=True) + EPS) * g


HEAD_COLS = 2 * HEAD_DK + 2 * HEAD_DV


def _weight_spec(w, layer, block, index_map):
    if w.ndim == 3:
        return pl.BlockSpec((None,) + block, lambda *g: (layer,) + index_map(*g))
    return pl.BlockSpec(block, index_map)


def _bf16_copy(w, block, index_map):
    return jax.ShapeDtypeStruct(w.shape[1:], BF16), pl.BlockSpec(block, index_map)


def _load_weights(w_refs, copy_refs):
    ws = [w_ref[...].astype(BF16) for w_ref in w_refs]
    for copy_ref, w in zip(copy_refs, ws):
        copy_ref[...] = w
    return ws


def _qkvg_kernel(x_ref, g_ref, wq_ref, wk_ref, wv_ref, wg_ref, cos_ref, sin_ref, o_ref, *rest):
    h_ref = rest[-1]

    @pl.when(pl.program_id(1) == 0)
    def _():
        h_ref[...] = _rms(x_ref[...], g_ref[...]).astype(BF16)

    wq, wk, wv, wg = _load_weights((wq_ref, wk_ref, wv_ref, wg_ref), rest[:-1])
    h = h_ref[...]
    cos = cos_ref[...]
    sin = sin_ref[...]

    for a, w, scale in ((0, wq, 1.0), (HEAD_DK, wk, HEAD_DK ** -0.5)):
        acc = jnp.dot(h, w, preferred_element_type=F32)
        x1 = acc[:, :HALF_DK]
        x2 = acc[:, HALF_DK:]
        o_ref[:, a:a + HALF_DK] = ((x1 * cos - x2 * sin) * scale).astype(BF16)
        o_ref[:, a + HALF_DK:a + HEAD_DK] = ((x2 * cos + x1 * sin) * scale).astype(BF16)

    a = 2 * HEAD_DK
    o_ref[:, a:a + HEAD_DV] = jnp.dot(h, wv, preferred_element_type=F32).astype(BF16)
    a += HEAD_DV
    acc = jnp.dot(h, wg, preferred_element_type=F32)
    o_ref[:, a:a + HEAD_DV] = (acc * jax.nn.sigmoid(acc)).astype(BF16)


def _qkvg(x, gain, wq, wk, wv, wg, cos, sin, *, layer, tm):
    T = x.shape[0]
    n_pos_blocks = cos.shape[0] // tm
    emit = wq.ndim == 3
    assert not emit or T == tm
    wmap = lambda i, j: (0, j)
    blocks = ((D_MODEL, HEAD_DK), (D_MODEL, HEAD_DK), (D_MODEL, HEAD_DV), (D_MODEL, HEAD_DV))
    ws = (wq, wk, wv, wg)
    copies = [_bf16_copy(w, blk, wmap) for w, blk in zip(ws, blocks)] if emit else []
    res = pl.pallas_call(
        _qkvg_kernel,
        out_shape=[jax.ShapeDtypeStruct((T, N_HEADS * HEAD_COLS), BF16)] + [c[0] for c in copies],
        grid=(T // tm, N_HEADS),
        in_specs=[
            pl.BlockSpec((tm, D_MODEL), lambda i, j: (i, 0)),
            pl.BlockSpec((1, D_MODEL), lambda i, j: (0, 0)),
            *[_weight_spec(w, layer, blk, wmap) for w, blk in zip(ws, blocks)],
            pl.BlockSpec((tm, HALF_DK), lambda i, j: (i % n_pos_blocks, 0)),
            pl.BlockSpec((tm, HALF_DK), lambda i, j: (i % n_pos_blocks, 0)),
        ],
        out_specs=[pl.BlockSpec((tm, HEAD_COLS), lambda i, j: (i, j))] + [c[1] for c in copies],
        scratch_shapes=[pltpu.VMEM((tm, D_MODEL), BF16)],
        compiler_params=_params("parallel", "arbitrary"),
        name="qkvg_proj",
    )(x, gain, wq, wk, wv, wg, cos, sin)
    return res if emit else res[0]


def _retention_kernel(qkvg_ref, s0_ref, d_ref, qd_ref, kd_ref, sd_ref, gr_ref, z_ref, s_ref, *, nblk, bs, heads):
    for hh in range(heads):
        col = hh * HEAD_COLS
        s_ref[0, hh] = s0_ref[0, hh]
        dmat = d_ref[hh]
        qd = jnp.concatenate([qd_ref[hh]] * (HEAD_DV // LANES), axis=-1)
        kd = jnp.concatenate([kd_ref[hh]] * (HEAD_DK // LANES), axis=-1)
        sd = sd_ref[hh]
        gr = gr_ref[hh]

        for c in range(nblk):
            rows = pl.ds(c * bs, bs)
            q = qkvg_ref[rows, col:col + HEAD_DK]
            k = qkvg_ref[rows, col + HEAD_DK:col + 2 * HEAD_DK]
            v = qkvg_ref[rows, col + 2 * HEAD_DK:col + 2 * HEAD_DK + HEAD_DV]
            gate = qkvg_ref[rows, col + 2 * HEAD_DK + HEAD_DV:col + HEAD_COLS]
            S = s_ref[0, hh]
            s = lax.dot_general(q, k, (((1,), (1,)), ((), ())), preferred_element_type=F32)
            p = (s * dmat).astype(BF16)
            o = jnp.dot(p, v, preferred_element_type=F32)
            o = o + jnp.dot(q, S.astype(BF16), preferred_element_type=F32) * qd
            kk = (k.astype(F32) * kd).astype(BF16)
            upd = lax.dot_general(kk, v, (((0,), (0,)), ((), ())), preferred_element_type=F32)
            s_ref[0, hh] = S * sd + upd
            z_ref[rows, hh * HEAD_DV:(hh + 1) * HEAD_DV] = (gate.astype(F32) * _rms(o, gr)).astype(BF16)


def _retention(qkvg, s0, tabs, g_ret, *, layer, s0_layer, nb, seq, bs, row0, heads):
    dmat, qd, kd, sd = tabs
    rb0 = row0 // seq
    if s0.shape[1] == nb:
        s0_map = lambda b, h: (s0_layer, b, h, 0, 0)
    else:
        s0_map = lambda b, h: (s0_layer, 0, h, 0, 0)
    z, s_new = pl.pallas_call(
        functools.partial(_retention_kernel, nblk=seq // bs, bs=bs, heads=heads),
        out_shape=(jax.ShapeDtypeStruct((nb * seq, HV), BF16),
                   jax.ShapeDtypeStruct((nb, N_HEADS, HEAD_DK, HEAD_DV), F32)),
        grid=(nb, N_HEADS // heads),
        in_specs=[
            pl.BlockSpec((seq, heads * HEAD_COLS), lambda b, h: (rb0 + b, h)),
            pl.BlockSpec((None, 1, heads, HEAD_DK, HEAD_DV), s0_map),
            pl.BlockSpec((heads, bs, bs), lambda b, h: (h, 0, 0)),
            pl.BlockSpec((heads, bs, LANES), lambda b, h: (h, 0, 0)),
            pl.BlockSpec((heads, bs, LANES), lambda b, h: (h, 0, 0)),
            pl.BlockSpec((heads, 1, HEAD_DV), lambda b, h: (h, 0, 0)),
            pl.BlockSpec((None, heads, 1, HEAD_DV), lambda b, h: (layer, h, 0, 0)),
        ],
        out_specs=(pl.BlockSpec((seq, heads * HEAD_DV), lambda b, h: (b, h)),
                   pl.BlockSpec((1, heads, HEAD_DK, HEAD_DV), lambda b, h: (b, h, 0, 0))),
        compiler_params=_params("parallel", "parallel"),
        name="retention",
    )(qkvg, s0, dmat, qd, kd, sd, g_ret)
    return z, s_new


def _decay_tables(bs, chunk):
    lg = jnp.log1p(-jnp.exp2(-5.0 - jnp.arange(N_HEADS, dtype=F32)))[:, None, None]
    idx = jnp.arange(bs, dtype=F32)
    dist = idx[:, None] - idx[None, :]
    ci = jnp.arange(bs) // chunk
    visible = ci[None, :] <= ci[:, None]
    same = ci[None, :] == ci[:, None]
    dmat = jnp.where(visible[None], jnp.exp(lg * jnp.where(same, jnp.abs(dist), dist)[None]), 0.0)
    qd = jnp.broadcast_to(jnp.exp(lg * (idx[None, :, None] + 1.0)), (N_HEADS, bs, LANES))
    kd = jnp.broadcast_to(jnp.exp(lg * (bs - 1.0 - idx)[None, :, None]), (N_HEADS, bs, LANES))
    sd = jnp.broadcast_to(jnp.exp(lg * bs), (N_HEADS, 1, HEAD_DV))
    return dmat.astype(F32), qd.astype(F32), kd.astype(F32), sd.astype(F32)


def _matmul_residual_kernel(a_ref, w_ref, x_ref, o_ref, *copy_refs):
    (w,) = _load_weights((w_ref,), copy_refs)
    o_ref[...] = x_ref[...] + jnp.dot(a_ref[...], w, preferred_element_type=F32)


def _matmul_residual(a, w, x, *, layer, tm, tn):
    T, K = a.shape
    N = w.shape[-1]
    emit = w.ndim == 3
    assert not emit or T == tm
    wmap = lambda i, j: (0, j)
    copies = [_bf16_copy(w, (K, tn), wmap)] if emit else []
    res = pl.pallas_call(
        _matmul_residual_kernel,
        out_shape=[jax.ShapeDtypeStruct((T, N), F32)] + [c[0] for c in copies],
        grid=(T // tm, N // tn),
        in_specs=[
            pl.BlockSpec((tm, K), lambda i, j: (i, 0)),
            _weight_spec(w, layer, (K, tn), wmap),
            pl.BlockSpec((tm, tn), lambda i, j: (i, j)),
        ],
        out_specs=[pl.BlockSpec((tm, tn), lambda i, j: (i, j))] + [c[1] for c in copies],
        compiler_params=_params("parallel", "arbitrary"),
        name="out_proj",
    )(a, w, x)
    return res if emit else res[0]


def _mlp_kernel(x_ref, g_ref, wu_ref, wd_ref, gf_ref, o_ref, *rest, final_norm):
    f = pl.program_id(1)
    h_ref = rest[-1]

    @pl.when(f == 0)
    def _():
        x = x_ref[...]
        h_ref[...] = _rms(x, g_ref[...]).astype(BF16)
        o_ref[...] = x

    wu, wd = _load_weights((wu_ref, wd_ref), rest[:-1])
    u = jnp.dot(h_ref[...], wu, preferred_element_type=F32)
    r = jnp.maximum(u, 0.0)
    o_ref[...] += jnp.dot((r * r).astype(BF16), wd, preferred_element_type=F32)

    if final_norm:
        @pl.when(f == pl.num_programs(1) - 1)
        def _():
            o_ref[...] = _rms(o_ref[...], gf_ref[...])


def _mlp(x, gain, w_up, w_down, gain_final, *, layer, tm, tf, final_norm):
    T = x.shape[0]
    emit = w_up.ndim == 3
    assert not emit or T == tm
    up_args = ((D_MODEL, tf), lambda i, f: (0, f))
    down_args = ((tf, D_MODEL), lambda i, f: (f, 0))
    copies = [_bf16_copy(w_up, *up_args), _bf16_copy(w_down, *down_args)] if emit else []
    res = pl.pallas_call(
        functools.partial(_mlp_kernel, final_norm=final_norm),
        out_shape=[jax.ShapeDtypeStruct((T, D_MODEL), F32)] + [c[0] for c in copies],
        grid=(T // tm, D_FF // tf),
        in_specs=[
            pl.BlockSpec((tm, D_MODEL), lambda i, f: (i, 0)),
            pl.BlockSpec((1, D_MODEL), lambda i, f: (0, 0)),
            _weight_spec(w_up, layer, *up_args),
            _weight_spec(w_down, layer, *down_args),
            pl.BlockSpec((1, D_MODEL), lambda i, f: (0, 0)),
        ],
        out_specs=[pl.BlockSpec((tm, D_MODEL), lambda i, f: (i, 0))] + [c[1] for c in copies],
        scratch_shapes=[pltpu.VMEM((tm, D_MODEL), BF16)],
        compiler_params=_params("parallel", "arbitrary"),
        name="mlp",
    )(x, gain, w_up, w_down, gain_final)
    return res if emit else res[0]


def _glu_kernel(x_ref, g_ref, wa_ref, wg_ref, ba_ref, bg_ref, o_ref, *rest):
    h_ref = rest[-1]

    @pl.when(pl.program_id(1) == 0)
    def _():
        h_ref[...] = _rms(x_ref[...], g_ref[...]).astype(BF16)

    wa, wg = _load_weights((wa_ref, wg_ref), rest[:-1])
    h = h_ref[...]
    a = jnp.dot(h, wa, preferred_element_type=F32) + ba_ref[...]
    g = jnp.dot(h, wg, preferred_element_type=F32) + bg_ref[...]
    o_ref[...] = a * jax.nn.sigmoid(g)


def _glu(x, gain, w, b, *, layer, tm, tn):
    T = x.shape[0]
    nj = D_MODEL // tn
    emit = not isinstance(w, tuple)
    assert not emit or T == tm
    half = pl.BlockSpec((D_MODEL, tn), lambda i, j: (0, j))
    if emit:
        w_args = (w, w)
        w_specs = [pl.BlockSpec((None, D_MODEL, tn), lambda i, j: (layer, 0, j)),
                   pl.BlockSpec((None, D_MODEL, tn), lambda i, j: (layer, 0, nj + j))]
        copies = [(jax.ShapeDtypeStruct((D_MODEL, D_MODEL), BF16), half)] * 2
    else:
        w_args = w
        w_specs = [half, half]
        copies = []
    res = pl.pallas_call(
        _glu_kernel,
        out_shape=[jax.ShapeDtypeStruct((T, D_MODEL), F32)] + [c[0] for c in copies],
        grid=(T // tm, nj),
        in_specs=[
            pl.BlockSpec((tm, D_MODEL), lambda i, j: (i, 0)),
            pl.BlockSpec((1, D_MODEL), lambda i, j: (0, 0)),
            *w_specs,
            pl.BlockSpec((1, tn), lambda i, j: (0, j)),
            pl.BlockSpec((1, tn), lambda i, j: (0, nj + j)),
        ],
        out_specs=[pl.BlockSpec((tm, tn), lambda i, j: (i, j))] + [c[1] for c in copies],
        scratch_shapes=[pltpu.VMEM((tm, D_MODEL), BF16)],
        compiler_params=_params("parallel", "arbitrary"),
        name="glu_proj",
    )(x, gain, *w_args, b, b)
    return (res[0], (res[1], res[2])) if emit else res[0]


def _conv_kernel(u_ref, x_ref, tail_ref, wdw_ref, bdw_ref, lng_ref, lnb_ref, w2_ref, b2_ref,
                 o_ref, tail_out_ref, win_ref, c_ref, *, tm, rows):
    t = pl.program_id(1)
    n_cb = D_MODEL // LANES

    @pl.when(t == 0)
    def _():
        win_ref[:, 0:WIN_PAD - CONV_TAIL, :] = jnp.zeros((n_cb, WIN_PAD - CONV_TAIL, LANES), F32)
        for cb in range(n_cb):
            win_ref[cb, WIN_PAD - CONV_TAIL:WIN_PAD, :] = tail_ref[0, :, cb * LANES:(cb + 1) * LANES]

    @pl.when(t > 0)
    def _():
        win_ref[:, 0:WIN_PAD, :] = win_ref[:, tm:tm + WIN_PAD, :]

    for cb in range(n_cb):
        win_ref[cb, WIN_PAD:WIN_PAD + tm, :] = u_ref[:, cb * LANES:(cb + 1) * LANES]

    def col_block(cb, carry):
        cols = pl.ds(pl.multiple_of(cb * LANES, LANES), LANES)
        groups = rows // SUBLANES
        bias = jnp.broadcast_to(bdw_ref[:, cols], (groups, LANES))
        w = [jnp.broadcast_to(wdw_ref[CONV_TAIL - j:CONV_WIDTH - j, cols], (groups, LANES))
             for j in range(CONV_WIDTH)]
        for r0 in range(0, tm, rows):
            acc = [bias] * SUBLANES
            for d in range(-CONV_TAIL, SUBLANES):
                x = win_ref[cb, pl.ds(WIN_PAD + r0 + d, groups, stride=SUBLANES), :]
                for m in range(max(0, d), min(SUBLANES - 1, d + CONV_TAIL) + 1):
                    acc[m] = acc[m] + w[m - d] * x
            for m in range(SUBLANES):
                c_ref[cb, pl.ds(r0 + m, groups, stride=SUBLANES), :] = acc[m]
        return carry

    lax.fori_loop(0, n_cb, col_block, 0)

    c = jnp.concatenate([c_ref[cb] for cb in range(n_cb)], axis=-1)
    xc = c - jnp.mean(c, axis=-1, keepdims=True)
    y = xc * lax.rsqrt(jnp.mean(xc * xc, axis=-1, keepdims=True) + EPS) * lng_ref[...] + lnb_ref[...]
    a = (y * jax.nn.sigmoid(y)).astype(BF16)
    o_ref[...] = x_ref[...] + jnp.dot(a, w2_ref[...], preferred_element_type=F32) + b2_ref[...]

    @pl.when(t == pl.num_programs(1) - 1)
    def _():
        for cb in range(n_cb):
            tail_out_ref[0, :, cb * LANES:(cb + 1) * LANES] = win_ref[cb, WIN_PAD + tm - CONV_TAIL:WIN_PAD + tm, :]


def _conv(u, x, tail, w_dw, b_dw, ln_g, ln_b, w2, b2, *, layer, tail_layer, nb, seq, tm, row0):
    nt = seq // tm
    tb0 = row0 // tm
    if tail.shape[1] == nb:
        tail_map = lambda b, t: (tail_layer, b, 0, 0)
    else:
        tail_map = lambda b, t: (tail_layer, 0, 0, 0)
    vec = pl.BlockSpec((1, D_MODEL), lambda b, t: (0, 0))
    return pl.pallas_call(
        functools.partial(_conv_kernel, tm=tm, rows=min(tm, 64)),
        out_shape=(jax.ShapeDtypeStruct((nb * seq, D_MODEL), F32),
                   jax.ShapeDtypeStruct((nb, CONV_TAIL, D_MODEL), F32)),
        grid=(nb, nt),
        in_specs=[
            pl.BlockSpec((tm, D_MODEL), lambda b, t: (tb0 + b * nt + t, 0)),
            pl.BlockSpec((tm, D_MODEL), lambda b, t: (tb0 + b * nt + t, 0)),
            pl.BlockSpec((None, 1, CONV_TAIL, D_MODEL), tail_map),
            pl.BlockSpec((CONV_WIDTH, D_MODEL), lambda b, t: (0, 0)),
            vec, vec, vec,
            pl.BlockSpec((None, D_MODEL, D_MODEL), lambda b, t: (layer, 0, 0), pipeline_mode=pl.Buffered(1)),
            vec,
        ],
        out_specs=(pl.BlockSpec((tm, D_MODEL), lambda b, t: (b * nt + t, 0)),
                   pl.BlockSpec((1, CONV_TAIL, D_MODEL), lambda b, t: (b, 0, 0))),
        scratch_shapes=[pltpu.VMEM((D_MODEL // LANES, WIN_PAD + tm, LANES), F32),
                        pltpu.VMEM((D_MODEL // LANES, tm, LANES), F32)],
        compiler_params=_params("parallel", "arbitrary"),
        name="conv_module",
    )(u, x, tail, w_dw, b_dw, ln_g, ln_b, w2, b2)


def _rope_tables(pos):
    inv = ROPE_BASE ** (-jnp.arange(HALF_DK, dtype=F32) / HALF_DK)
    ang = pos.astype(F32)[:, None] * inv[None, :]
    return jnp.cos(ang), jnp.sin(ang)


def _row(v):
    return v.reshape(1, -1)


def kernel(x_prompt, x_sample, state_ret, cache_conv, meta, norm_mix, w_q, w_k, w_v, w_g, w_o, g_ret,
           w_pw1, b_pw1, w_dw, b_dw, ln_g, ln_b, w_pw2, b_pw2, norm_mlp, w_up, w_down, norm_final):
    B, SEQ, _ = x_prompt.shape
    DB, DSEQ, _ = x_sample.shape
    n_s = DB * DSEQ

    xs = jnp.concatenate([x_sample.reshape(n_s, D_MODEL), meta.astype(F32)], axis=0)
    xp = x_prompt.reshape(B * SEQ, D_MODEL)

    pos_s = jnp.concatenate([
        jnp.tile(PAST_LEN + N_META + jnp.arange(DSEQ), DB), jnp.arange(N_META)])
    rope_s = _rope_tables(pos_s)
    rope_p = _rope_tables(N_META + jnp.arange(SEQ))

    bs_p = 256
    tabs_p = _decay_tables(bs_p, CHUNK)
    tabs_s = _decay_tables(DSEQ, DSEQ)
    tabs_m = _decay_tables(N_META, N_META)
    zero_state = jnp.zeros((1, 1, N_HEADS, HEAD_DK, HEAD_DV), F32)
    zero_tail = jnp.zeros((1, 1, CONV_TAIL, D_MODEL), F32)

    w2 = w_pw2.astype(BF16)
    gr = g_ret.reshape(-1, N_HEADS, 1, HEAD_DV)

    n_small = n_s + N_META
    ret_p, ret_s, conv_p, conv_s = [], [], [], []
    for l in range(DEPTH):
        i = l // 2
        gmix = _row(norm_mix[l])
        if l % 2 == 0:
            qs, wq, wk, wv, wg = _qkvg(xs, gmix, w_q, w_k, w_v, w_g, *rope_s, layer=i, tm=n_small)
            z_s, st_s = _retention(qs, state_ret, tabs_s, gr, layer=i, s0_layer=i,
                                   nb=DB, seq=DSEQ, bs=DSEQ, row0=0, heads=N_HEADS)
            z_m, st_m = _retention(qs, zero_state, tabs_m, gr, layer=i, s0_layer=0,
                                   nb=1, seq=N_META, bs=N_META, row0=n_s, heads=N_HEADS)
            xs, wo = _matmul_residual(jnp.concatenate([z_s, z_m], axis=0), w_o, xs, layer=i, tm=n_small, tn=512)
            ret_s.append(st_s)

            qp = _qkvg(xp, gmix, wq, wk, wv, wg, *rope_p, layer=i, tm=1024)
            z_p, st_p = _retention(qp, st_m[None], tabs_p, gr, layer=i, s0_layer=0,
                                   nb=B, seq=SEQ, bs=bs_p, row0=0, heads=1)
            xp = _matmul_residual(z_p, wo, xp, layer=i, tm=1024, tn=1024)
            ret_p.append(st_p)
        else:
            cargs = (w_dw[i], _row(b_dw[i]), _row(ln_g[i]), _row(ln_b[i]), w2, _row(b_pw2[i]))

            us, w1 = _glu(xs, gmix, w_pw1, _row(b_pw1[i]), layer=i, tm=n_small, tn=512)
            x_s, tail_s = _conv(us, xs, cache_conv, *cargs, layer=i, tail_layer=i,
                                nb=DB, seq=DSEQ, tm=DSEQ, row0=0)
            x_m, tail_m = _conv(us, xs, zero_tail, *cargs, layer=i, tail_layer=0,
                                nb=1, seq=N_META, tm=N_META, row0=n_s)
            xs = jnp.concatenate([x_s, x_m], axis=0)
            conv_s.append(tail_s)

            up = _glu(xp, gmix, w1, _row(b_pw1[i]), layer=i, tm=1024, tn=1024)
            xp, tail_p = _conv(up, xp, tail_m[None], *cargs, layer=i, tail_layer=0,
                               nb=B, seq=SEQ, tm=512, row0=0)
            conv_p.append(tail_p)

        last = l == DEPTH - 1
        xs, wu, wd = _mlp(xs, _row(norm_mlp[l]), w_up, w_down, _row(norm_final),
                          layer=l, tm=n_small, tf=512, final_norm=last)
        xp = _mlp(xp, _row(norm_mlp[l]), wu, wd, _row(norm_final), layer=l, tm=512, tf=1024, final_norm=last)

    return (xp.reshape(B, SEQ, D_MODEL), xs[:n_s].reshape(DB, DSEQ, D_MODEL),
            jnp.stack(ret_p), jnp.stack(conv_p), jnp.stack(ret_s), jnp.stack(conv_s))
```

```python
import functools

import jax
import jax.numpy as jnp
from jax import lax
from jax.experimental import pallas as pl
from jax.experimental.pallas import tpu as pltpu

D_MODEL = 2048
DEPTH = 4
CHUNK = 64
N_META = 16
N_HEADS = 8
HEAD_DK = D_MODEL // N_HEADS
HEAD_DV = 2 * HEAD_DK
HALF_DK = HEAD_DK // 2
HV = N_HEADS * HEAD_DV
CONV_WIDTH = 31
CONV_TAIL = CONV_WIDTH - 1
D_FF = 4 * D_MODEL
ROPE_BASE = 10000.0
EPS = 1e-6
PAST_LEN = 4096

LANES = 128
SUBLANES = 8
WIN_PAD = 32
VMEM_LIMIT = 56 * 1024 * 1024

F32 = jnp.float32
BF16 = jnp.bfloat16


def _params(*sem):
    return pltpu.CompilerParams(dimension_semantics=sem, vmem_limit_bytes=VMEM_LIMIT)


def _rms(x, g):
    return x * lax.rsqrt(jnp.mean(x * x, axis=-1, keepdims=True) + EPS) * g


HEAD_COLS = 2 * HEAD_DK + 2 * HEAD_DV


def _weight_spec(w, layer, block, index_map):
    if w.ndim == 3:
        return pl.BlockSpec((None,) + block, lambda *g: (layer,) + index_map(*g))
    return pl.BlockSpec(block, index_map)


def _bf16_copy(w, block, index_map):
    return jax.ShapeDtypeStruct(w.shape[1:], BF16), pl.BlockSpec(block, index_map)


def _load_weights(w_refs, copy_refs):
    ws = [w_ref[...].astype(BF16) for w_ref in w_refs]
    for copy_ref, w in zip(copy_refs, ws):
        copy_ref[...] = w
    return ws


def _qkvg_kernel(x_ref, g_ref, wq_ref, wk_ref, wv_ref, wg_ref, cos_ref, sin_ref, o_ref, *rest):
    h_ref = rest[-1]

    @pl.when(pl.program_id(1) == 0)
    def _():
        h_ref[...] = _rms(x_ref[...], g_ref[...]).astype(BF16)

    wq, wk, wv, wg = _load_weights((wq_ref, wk_ref, wv_ref, wg_ref), rest[:-1])
    h = h_ref[...]
    cos = cos_ref[...]
    sin = sin_ref[...]

    for a, w, scale in ((0, wq, 1.0), (HEAD_DK, wk, HEAD_DK ** -0.5)):
        acc = jnp.dot(h, w, preferred_element_type=F32)
        x1 = acc[:, :HALF_DK]
        x2 = acc[:, HALF_DK:]
        o_ref[:, a:a + HALF_DK] = ((x1 * cos - x2 * sin) * scale).astype(BF16)
        o_ref[:, a + HALF_DK:a + HEAD_DK] = ((x2 * cos + x1 * sin) * scale).astype(BF16)

    a = 2 * HEAD_DK
    o_ref[:, a:a + HEAD_DV] = jnp.dot(h, wv, preferred_element_type=F32).astype(BF16)
    a += HEAD_DV
    acc = jnp.dot(h, wg, preferred_element_type=F32)
    o_ref[:, a:a + HEAD_DV] = (acc * jax.nn.sigmoid(acc)).astype(BF16)


def _qkvg(x, gain, wq, wk, wv, wg, cos, sin, *, layer, tm):
    T = x.shape[0]
    n_pos_blocks = cos.shape[0] // tm
    emit = wq.ndim == 3
    assert not emit or T == tm
    wmap = lambda i, j: (0, j)
    blocks = ((D_MODEL, HEAD_DK), (D_MODEL, HEAD_DK), (D_MODEL, HEAD_DV), (D_MODEL, HEAD_DV))
    ws = (wq, wk, wv, wg)
    copies = [_bf16_copy(w, blk, wmap) for w, blk in zip(ws, blocks)] if emit else []
    res = pl.pallas_call(
        _qkvg_kernel,
        out_shape=[jax.ShapeDtypeStruct((T, N_HEADS * HEAD_COLS), BF16)] + [c[0] for c in copies],
        grid=(T // tm, N_HEADS),
        in_specs=[
            pl.BlockSpec((tm, D_MODEL), lambda i, j: (i, 0)),
            pl.BlockSpec((1, D_MODEL), lambda i, j: (0, 0)),
            *[_weight_spec(w, layer, blk, wmap) for w, blk in zip(ws, blocks)],
            pl.BlockSpec((tm, HALF_DK), lambda i, j: (i % n_pos_blocks, 0)),
            pl.BlockSpec((tm, HALF_DK), lambda i, j: (i % n_pos_blocks, 0)),
        ],
        out_specs=[pl.BlockSpec((tm, HEAD_COLS), lambda i, j: (i, j))] + [c[1] for c in copies],
        scratch_shapes=[pltpu.VMEM((tm, D_MODEL), BF16)],
        compiler_params=_params("parallel", "arbitrary"),
        name="qkvg_proj",
    )(x, gain, wq, wk, wv, wg, cos, sin)
    return res if emit else res[0]


def _retention_kernel(qkvg_ref, s0_ref, d_ref, qd_ref, kd_ref, sd_ref, gr_ref, *rest, nblk, bs, heads):
    z_ref, s_ref = rest[-2:]
    for hh in range(heads):
        col = hh * HEAD_COLS
        s_ref[0, hh] = s0_ref[0, hh]
        dmat = d_ref[hh]
        qd = jnp.concatenate([qd_ref[hh]] * (HEAD_DV // LANES), axis=-1)
        kd = jnp.concatenate([kd_ref[hh]] * (HEAD_DK // LANES), axis=-1)
        sd = sd_ref[hh]
        gr = gr_ref[hh]

        for c in range(nblk):
            rows = pl.ds(c * bs, bs)
            q = qkvg_ref[rows, col:col + HEAD_DK]
            k = qkvg_ref[rows, col + HEAD_DK:col + 2 * HEAD_DK]
            v = qkvg_ref[rows, col + 2 * HEAD_DK:col + 2 * HEAD_DK + HEAD_DV]
            gate = qkvg_ref[rows, col + 2 * HEAD_DK + HEAD_DV:col + HEAD_COLS]
            S = s_ref[0, hh]
            s = lax.dot_general(q, k, (((1,), (1,)), ((), ())), preferred_element_type=F32)
            p = (s * dmat).astype(BF16)
            o = jnp.dot(p, v, preferred_element_type=F32)
            o = o + jnp.dot(q, S.astype(BF16), preferred_element_type=F32) * qd
            kk = (k.astype(F32) * kd).astype(BF16)
            upd = lax.dot_general(kk, v, (((0,), (0,)), ((), ())), preferred_element_type=F32)
            s_ref[0, hh] = S * sd + upd
            z_ref[rows, hh * HEAD_DV:(hh + 1) * HEAD_DV] = (gate.astype(F32) * _rms(o, gr)).astype(BF16)


def _retention(qkvg, s0, tabs, g_ret, *, layer, s0_layer, nb, seq, bs, row0, heads, states=None):
    dmat, qd, kd, sd = tabs
    rb0 = row0 // seq
    if s0.shape[1] == nb:
        s0_map = lambda b, h: (s0_layer, b, h, 0, 0)
    else:
        s0_map = lambda b, h: (s0_layer, 0, h, 0, 0)
    in_specs = [
        pl.BlockSpec((seq, heads * HEAD_COLS), lambda b, h: (rb0 + b, h)),
        pl.BlockSpec((None, 1, heads, HEAD_DK, HEAD_DV), s0_map),
        pl.BlockSpec((heads, bs, bs), lambda b, h: (h, 0, 0)),
        pl.BlockSpec((heads, bs, LANES), lambda b, h: (h, 0, 0)),
        pl.BlockSpec((heads, bs, LANES), lambda b, h: (h, 0, 0)),
        pl.BlockSpec((heads, 1, HEAD_DV), lambda b, h: (h, 0, 0)),
        pl.BlockSpec((None, heads, 1, HEAD_DV), lambda b, h: (layer, h, 0, 0)),
    ]
    args = [qkvg, s0, dmat, qd, kd, sd, g_ret]
    if states is None:
        state_shape = jax.ShapeDtypeStruct((nb, N_HEADS, HEAD_DK, HEAD_DV), F32)
        state_spec = pl.BlockSpec((1, heads, HEAD_DK, HEAD_DV), lambda b, h: (b, h, 0, 0))
        aliases = {}
    else:
        state_shape = jax.ShapeDtypeStruct(states.shape, F32)
        state_spec = pl.BlockSpec((None, 1, heads, HEAD_DK, HEAD_DV), lambda b, h: (layer, b, h, 0, 0))
        in_specs.append(pl.BlockSpec(memory_space=pl.ANY))
        args.append(states)
        aliases = {len(args) - 1: 1}
    z, s_new = pl.pallas_call(
        functools.partial(_retention_kernel, nblk=seq // bs, bs=bs, heads=heads),
        out_shape=(jax.ShapeDtypeStruct((nb * seq, HV), BF16), state_shape),
        grid=(nb, N_HEADS // heads),
        in_specs=in_specs,
        out_specs=(pl.BlockSpec((seq, heads * HEAD_DV), lambda b, h: (b, h)), state_spec),
        input_output_aliases=aliases,
        compiler_params=_params("parallel", "parallel"),
        name="retention",
    )(*args)
    return z, s_new


def _decay_tables(bs, chunk):
    lg = jnp.log1p(-jnp.exp2(-5.0 - jnp.arange(N_HEADS, dtype=F32)))[:, None, None]
    idx = jnp.arange(bs, dtype=F32)
    dist = idx[:, None] - idx[None, :]
    ci = jnp.arange(bs) // chunk
    visible = ci[None, :] <= ci[:, None]
    same = ci[None, :] == ci[:, None]
    dmat = jnp.where(visible[None], jnp.exp(lg * jnp.where(same, jnp.abs(dist), dist)[None]), 0.0)
    qd = jnp.broadcast_to(jnp.exp(lg * (idx[None, :, None] + 1.0)), (N_HEADS, bs, LANES))
    kd = jnp.broadcast_to(jnp.exp(lg * (bs - 1.0 - idx)[None, :, None]), (N_HEADS, bs, LANES))
    sd = jnp.broadcast_to(jnp.exp(lg * bs), (N_HEADS, 1, HEAD_DV))
    return dmat.astype(F32), qd.astype(F32), kd.astype(F32), sd.astype(F32)


def _matmul_residual_kernel(a_ref, w_ref, x_ref, o_ref, *copy_refs):
    (w,) = _load_weights((w_ref,), copy_refs)
    o_ref[...] = x_ref[...] + jnp.dot(a_ref[...], w, preferred_element_type=F32)


def _matmul_residual(a, w, x, *, layer, tm, tn):
    T, K = a.shape
    N = w.shape[-1]
    emit = w.ndim == 3
    assert not emit or T == tm
    wmap = lambda i, j: (0, j)
    copies = [_bf16_copy(w, (K, tn), wmap)] if emit else []
    res = pl.pallas_call(
        _matmul_residual_kernel,
        out_shape=[jax.ShapeDtypeStruct((T, N), F32)] + [c[0] for c in copies],
        grid=(T // tm, N // tn),
        in_specs=[
            pl.BlockSpec((tm, K), lambda i, j: (i, 0)),
            _weight_spec(w, layer, (K, tn), wmap),
            pl.BlockSpec((tm, tn), lambda i, j: (i, j)),
        ],
        out_specs=[pl.BlockSpec((tm, tn), lambda i, j: (i, j))] + [c[1] for c in copies],
        compiler_params=_params("parallel", "arbitrary"),
        name="out_proj",
    )(a, w, x)
    return res if emit else res[0]


def _mlp_kernel(x_ref, g_ref, wu_ref, wd_ref, gf_ref, o_ref, *rest, final_norm):
    f = pl.program_id(1)
    h_ref = rest[-1]

    @pl.when(f == 0)
    def _():
        x = x_ref[...]
        h_ref[...] = _rms(x, g_ref[...]).astype(BF16)
        o_ref[...] = x

    wu, wd = _load_weights((wu_ref, wd_ref), rest[:-1])
    u = jnp.dot(h_ref[...], wu, preferred_element_type=F32)
    r = jnp.maximum(u, 0.0)
    o_ref[...] += jnp.dot((r * r).astype(BF16), wd, preferred_element_type=F32)

    if final_norm:
        @pl.when(f == pl.num_programs(1) - 1)
        def _():
            o_ref[...] = _rms(o_ref[...], gf_ref[...])


def _mlp(x, gain, w_up, w_down, gain_final, *, layer, tm, tf, final_norm):
    T = x.shape[0]
    emit = w_up.ndim == 3
    assert not emit or T == tm
    up_args = ((D_MODEL, tf), lambda i, f: (0, f))
    down_args = ((tf, D_MODEL), lambda i, f: (f, 0))
    copies = [_bf16_copy(w_up, *up_args), _bf16_copy(w_down, *down_args)] if emit else []
    res = pl.pallas_call(
        functools.partial(_mlp_kernel, final_norm=final_norm),
        out_shape=[jax.ShapeDtypeStruct((T, D_MODEL), F32)] + [c[0] for c in copies],
        grid=(T // tm, D_FF // tf),
        in_specs=[
            pl.BlockSpec((tm, D_MODEL), lambda i, f: (i, 0)),
            pl.BlockSpec((1, D_MODEL), lambda i, f: (0, 0)),
            _weight_spec(w_up, layer, *up_args),
            _weight_spec(w_down, layer, *down_args),
            pl.BlockSpec((1, D_MODEL), lambda i, f: (0, 0)),
        ],
        out_specs=[pl.BlockSpec((tm, D_MODEL), lambda i, f: (i, 0))] + [c[1] for c in copies],
        scratch_shapes=[pltpu.VMEM((tm, D_MODEL), BF16)],
        compiler_params=_params("parallel", "arbitrary"),
        name="mlp",
    )(x, gain, w_up, w_down, gain_final)
    return res if emit else res[0]


def _glu_kernel(x_ref, g_ref, wa_ref, wg_ref, ba_ref, bg_ref, o_ref, *rest):
    h_ref = rest[-1]

    @pl.when(pl.program_id(1) == 0)
    def _():
        h_ref[...] = _rms(x_ref[...], g_ref[...]).astype(BF16)

    wa, wg = _load_weights((wa_ref, wg_ref), rest[:-1])
    h = h_ref[...]
    a = jnp.dot(h, wa, preferred_element_type=F32) + ba_ref[...]
    g = jnp.dot(h, wg, preferred_element_type=F32) + bg_ref[...]
    o_ref[...] = a * jax.nn.sigmoid(g)


def _glu(x, gain, w, b, *, layer, tm, tn):
    T = x.shape[0]
    nj = D_MODEL // tn
    emit = not isinstance(w, tuple)
    assert not emit or T == tm
    half = pl.BlockSpec((D_MODEL, tn), lambda i, j: (0, j))
    if emit:
        w_args = (w, w)
        w_specs = [pl.BlockSpec((None, D_MODEL, tn), lambda i, j: (layer, 0, j)),
                   pl.BlockSpec((None, D_MODEL, tn), lambda i, j: (layer, 0, nj + j))]
        copies = [(jax.ShapeDtypeStruct((D_MODEL, D_MODEL), BF16), half)] * 2
    else:
        w_args = w
        w_specs = [half, half]
        copies = []
    res = pl.pallas_call(
        _glu_kernel,
        out_shape=[jax.ShapeDtypeStruct((T, D_MODEL), F32)] + [c[0] for c in copies],
        grid=(T // tm, nj),
        in_specs=[
            pl.BlockSpec((tm, D_MODEL), lambda i, j: (i, 0)),
            pl.BlockSpec((1, D_MODEL), lambda i, j: (0, 0)),
            *w_specs,
            pl.BlockSpec((1, tn), lambda i, j: (0, j)),
            pl.BlockSpec((1, tn), lambda i, j: (0, nj + j)),
        ],
        out_specs=[pl.BlockSpec((tm, tn), lambda i, j: (i, j))] + [c[1] for c in copies],
        scratch_shapes=[pltpu.VMEM((tm, D_MODEL), BF16)],
        compiler_params=_params("parallel", "arbitrary"),
        name="glu_proj",
    )(x, gain, *w_args, b, b)
    return (res[0], (res[1], res[2])) if emit else res[0]


def _conv_kernel(u_ref, x_ref, tail_ref, wdw_ref, bdw_ref, lng_ref, lnb_ref, w2_ref, b2_ref,
                 o_ref, tail_out_ref, win_ref, c_ref, *, tm, rows):
    t = pl.program_id(1)
    n_cb = D_MODEL // LANES

    @pl.when(t == 0)
    def _():
        win_ref[:, 0:WIN_PAD - CONV_TAIL, :] = jnp.zeros((n_cb, WIN_PAD - CONV_TAIL, LANES), F32)
        for cb in range(n_cb):
            win_ref[cb, WIN_PAD - CONV_TAIL:WIN_PAD, :] = tail_ref[0, :, cb * LANES:(cb + 1) * LANES]

    @pl.when(t > 0)
    def _():
        win_ref[:, 0:WIN_PAD, :] = win_ref[:, tm:tm + WIN_PAD, :]

    for cb in range(n_cb):
        win_ref[cb, WIN_PAD:WIN_PAD + tm, :] = u_ref[:, cb * LANES:(cb + 1) * LANES]

    def col_block(cb, carry):
        cols = pl.ds(pl.multiple_of(cb * LANES, LANES), LANES)
        groups = rows // SUBLANES
        bias = jnp.broadcast_to(bdw_ref[:, cols], (groups, LANES))
        w = [jnp.broadcast_to(wdw_ref[CONV_TAIL - j:CONV_WIDTH - j, cols], (groups, LANES))
             for j in range(CONV_WIDTH)]
        for r0 in range(0, tm, rows):
            acc = [bias] * SUBLANES
            for d in range(-CONV_TAIL, SUBLANES):
                x = win_ref[cb, pl.ds(WIN_PAD + r0 + d, groups, stride=SUBLANES), :]
                for m in range(max(0, d), min(SUBLANES - 1, d + CONV_TAIL) + 1):
                    acc[m] = acc[m] + w[m - d] * x
            for m in range(SUBLANES):
                c_ref[cb, pl.ds(r0 + m, groups, stride=SUBLANES), :] = acc[m]
        return carry

    lax.fori_loop(0, n_cb, col_block, 0)

    c = jnp.concatenate([c_ref[cb] for cb in range(n_cb)], axis=-1)
    xc = c - jnp.mean(c, axis=-1, keepdims=True)
    y = xc * lax.rsqrt(jnp.mean(xc * xc, axis=-1, keepdims=True) + EPS) * lng_ref[...] + lnb_ref[...]
    a = (y * jax.nn.sigmoid(y)).astype(BF16)
    o_ref[...] = x_ref[...] + jnp.dot(a, w2_ref[...], preferred_element_type=F32) + b2_ref[...]

    @pl.when(t == pl.num_programs(1) - 1)
    def _():
        for cb in range(n_cb):
            tail_out_ref[0, :, cb * LANES:(cb + 1) * LANES] = win_ref[cb, WIN_PAD + tm - CONV_TAIL:WIN_PAD + tm, :]


def _conv(u, x, tail, w_dw, b_dw, ln_g, ln_b, w2, b2, *, layer, tail_layer, nb, seq, tm, row0):
    nt = seq // tm
    tb0 = row0 // tm
    if tail.shape[1] == nb:
        tail_map = lambda b, t: (tail_layer, b, 0, 0)
    else:
        tail_map = lambda b, t: (tail_layer, 0, 0, 0)
    vec = pl.BlockSpec((1, D_MODEL), lambda b, t: (0, 0))
    return pl.pallas_call(
        functools.partial(_conv_kernel, tm=tm, rows=min(tm, 64)),
        out_shape=(jax.ShapeDtypeStruct((nb * seq, D_MODEL), F32),
                   jax.ShapeDtypeStruct((nb, CONV_TAIL, D_MODEL), F32)),
        grid=(nb, nt),
        in_specs=[
            pl.BlockSpec((tm, D_MODEL), lambda b, t: (tb0 + b * nt + t, 0)),
            pl.BlockSpec((tm, D_MODEL), lambda b, t: (tb0 + b * nt + t, 0)),
            pl.BlockSpec((None, 1, CONV_TAIL, D_MODEL), tail_map),
            pl.BlockSpec((CONV_WIDTH, D_MODEL), lambda b, t: (0, 0)),
            vec, vec, vec,
            pl.BlockSpec((None, D_MODEL, D_MODEL), lambda b, t: (layer, 0, 0), pipeline_mode=pl.Buffered(1)),
            vec,
        ],
        out_specs=(pl.BlockSpec((tm, D_MODEL), lambda b, t: (b * nt + t, 0)),
                   pl.BlockSpec((1, CONV_TAIL, D_MODEL), lambda b, t: (b, 0, 0))),
        scratch_shapes=[pltpu.VMEM((D_MODEL // LANES, WIN_PAD + tm, LANES), F32),
                        pltpu.VMEM((D_MODEL // LANES, tm, LANES), F32)],
        compiler_params=_params("parallel", "arbitrary"),
        name="conv_module",
    )(u, x, tail, w_dw, b_dw, ln_g, ln_b, w2, b2)


def _rope_tables(pos):
    inv = ROPE_BASE ** (-jnp.arange(HALF_DK, dtype=F32) / HALF_DK)
    ang = pos.astype(F32)[:, None] * inv[None, :]
    return jnp.cos(ang), jnp.sin(ang)


def _row(v):
    return v.reshape(1, -1)


def kernel(x_prompt, x_sample, state_ret, cache_conv, meta, norm_mix, w_q, w_k, w_v, w_g, w_o, g_ret,
           w_pw1, b_pw1, w_dw, b_dw, ln_g, ln_b, w_pw2, b_pw2, norm_mlp, w_up, w_down, norm_final):
    B, SEQ, _ = x_prompt.shape
    DB, DSEQ, _ = x_sample.shape
    n_s = DB * DSEQ

    xs = jnp.concatenate([x_sample.reshape(n_s, D_MODEL), meta.astype(F32)], axis=0)
    xp = x_prompt.reshape(B * SEQ, D_MODEL)

    pos_s = jnp.concatenate([
        jnp.tile(PAST_LEN + N_META + jnp.arange(DSEQ), DB), jnp.arange(N_META)])
    rope_s = _rope_tables(pos_s)
    rope_p = _rope_tables(N_META + jnp.arange(SEQ))

    bs_p = 256
    tabs_p = _decay_tables(bs_p, CHUNK)
    tabs_s = _decay_tables(DSEQ, DSEQ)
    tabs_m = _decay_tables(N_META, N_META)
    zero_state = jnp.zeros((1, 1, N_HEADS, HEAD_DK, HEAD_DV), F32)
    zero_tail = jnp.zeros((1, 1, CONV_TAIL, D_MODEL), F32)

    w2 = w_pw2.astype(BF16)
    gr = g_ret.reshape(-1, N_HEADS, 1, HEAD_DV)

    n_small = n_s + N_META
    ret_p = jnp.zeros((state_ret.shape[0], B) + state_ret.shape[2:], F32)
    ret_s = jnp.zeros(state_ret.shape, F32)
    conv_p, conv_s = [], []
    for l in range(DEPTH):
        i = l // 2
        gmix = _row(norm_mix[l])
        if l % 2 == 0:
            qs, wq, wk, wv, wg = _qkvg(xs, gmix, w_q, w_k, w_v, w_g, *rope_s, layer=i, tm=n_small)
            z_s, ret_s = _retention(qs, state_ret, tabs_s, gr, layer=i, s0_layer=i,
                                    nb=DB, seq=DSEQ, bs=DSEQ, row0=0, heads=N_HEADS, states=ret_s)
            z_m, st_m = _retention(qs, zero_state, tabs_m, gr, layer=i, s0_layer=0,
                                   nb=1, seq=N_META, bs=N_META, row0=n_s, heads=N_HEADS)
            xs, wo = _matmul_residual(jnp.concatenate([z_s, z_m], axis=0), w_o, xs, layer=i, tm=n_small, tn=512)

            qp = _qkvg(xp, gmix, wq, wk, wv, wg, *rope_p, layer=i, tm=1024)
            z_p, ret_p = _retention(qp, st_m[None], tabs_p, gr, layer=i, s0_layer=0,
                                    nb=B, seq=SEQ, bs=bs_p, row0=0, heads=1, states=ret_p)
            xp = _matmul_residual(z_p, wo, xp, layer=i, tm=1024, tn=1024)
        else:
            cargs = (w_dw[i], _row(b_dw[i]), _row(ln_g[i]), _row(ln_b[i]), w2, _row(b_pw2[i]))

            us, w1 = _glu(xs, gmix, w_pw1, _row(b_pw1[i]), layer=i, tm=n_small, tn=512)
            x_s, tail_s = _conv(us, xs, cache_conv, *cargs, layer=i, tail_layer=i,
                                nb=DB, seq=DSEQ, tm=DSEQ, row0=0)
            x_m, tail_m = _conv(us, xs, zero_tail, *cargs, layer=i, tail_layer=0,
                                nb=1, seq=N_META, tm=N_META, row0=n_s)
            xs = jnp.concatenate([x_s, x_m], axis=0)
            conv_s.append(tail_s)

            up = _glu(xp, gmix, w1, _row(b_pw1[i]), layer=i, tm=1024, tn=1024)
            xp, tail_p = _conv(up, xp, tail_m[None], *cargs, layer=i, tail_layer=0,
                               nb=B, seq=SEQ, tm=512, row0=0)
            conv_p.append(tail_p)

        last = l == DEPTH - 1
        xs, wu, wd = _mlp(xs, _row(norm_mlp[l]), w_up, w_down, _row(norm_final),
                          layer=l, tm=n_small, tf=512, final_norm=last)
        xp = _mlp(xp, _row(norm_mlp[l]), wu, wd, _row(norm_final), layer=l, tm=512, tf=1024, final_norm=last)

    return (xp.reshape(B, SEQ, D_MODEL), xs[:n_s].reshape(DB, DSEQ, D_MODEL),
            ret_p, jnp.stack(conv_p), ret_s, jnp.stack(conv_s))
```
